```python
import jax
import jax.numpy as jnp
from jax import lax
import numpy as np


D_MODEL = 1024
BATCH = 8
SEQ = 2048
DEPTH = 4

CHUNK = 64
HEAD_DIM = 64
D_RWKV = 512
D_ATT = 512
D_MIX = D_RWKV + D_ATT
N_RWKV_HEADS = D_RWKV // HEAD_DIM
N_ATT_HEADS = D_ATT // HEAD_DIM
LORA_W = 64
LORA_A = 64
N_LEFT_CHUNKS = 8
BAND = (N_LEFT_CHUNKS + 1) * CHUNK
REL_CLIP = 128
N_REL = CHUNK + REL_CLIP
RMS_EPS = 1e-6
GN_EPS = 64e-5
D_SHIFT = 4 * D_RWKV + LORA_W + LORA_A
D_IN = D_SHIFT + 4 * D_ATT

kernel_name = 'hybrid_rwkv7_chunkattn_adaln_trunk'


def rms_norm(x, g):
    x32 = x.astype(jnp.float32)
    y = x32 * lax.rsqrt(jnp.mean(x32 * x32, axis=-1, keepdims=True) + RMS_EPS)
    return y.astype(x.dtype) * g


def token_shift(p, mu):
    prev = jnp.pad(p, ((0, 0), (1, 0), (0, 0)))[:, :-1]
    return p + mu * (prev - p)


def rwkv7_step(state, inp):
    r_t, w_t, k_t, v_t, a_t, b_t = inp
    sa = jnp.einsum('bhij,bhj->bhi', state, a_t)
    state = state * w_t[:, :, None, :] + sa[..., None] * b_t[:, :, None, :] + v_t[..., None] * k_t[:, :, None, :]
    y = jnp.einsum('bhij,bhj->bhi', state, r_t)
    return state, y


def rwkv7_mix(ps, w0, w2, a0, a2, k_k, k_a, r_k, lnx_g, lnx_b):
    b_, s_, _ = ps.shape
    f32 = jnp.float32
    r, k, v, g, wd, ad = jnp.split(ps, [D_RWKV, 2 * D_RWKV, 3 * D_RWKV, 4 * D_RWKV, 4 * D_RWKV + LORA_W], axis=-1)
    w_log = -jax.nn.softplus(-(w0 + jnp.tanh(wd) @ w2).astype(f32)) - 0.5
    decay = jnp.exp(-jnp.exp(w_log))
    a = jax.nn.sigmoid((a0 + ad @ a2).astype(f32))
    heads = lambda t: t.astype(f32).reshape(b_, s_, N_RWKV_HEADS, HEAD_DIM)
    r, k, v, decay, a = heads(r), heads(k), heads(v), heads(decay), heads(a)
    kk = k * k_k.astype(f32).reshape(N_RWKV_HEADS, HEAD_DIM)
    kk = kk / jnp.maximum(jnp.sqrt(jnp.sum(kk * kk, axis=-1, keepdims=True)), 1e-12)
    k = k * (1.0 + (a - 1.0) * k_a.astype(f32).reshape(N_RWKV_HEADS, HEAD_DIM))
    seq_first = lambda t: jnp.moveaxis(t, 1, 0)
    state0 = jnp.zeros((b_, N_RWKV_HEADS, HEAD_DIM, HEAD_DIM), f32)
    _, y = lax.scan(rwkv7_step, state0, (seq_first(r), seq_first(decay), seq_first(k), seq_first(v), seq_first(-kk), seq_first(kk * a)))
    y = jnp.moveaxis(y, 0, 1)
    mean = jnp.mean(y, axis=-1, keepdims=True)
    yc = y - mean
    y = yc * lax.rsqrt(jnp.mean(yc * yc, axis=-1, keepdims=True) + GN_EPS)
    y = y.reshape(b_, s_, D_RWKV) * lnx_g.astype(f32) + lnx_b.astype(f32)
    bonus = jnp.sum(r * k * r_k.astype(f32), axis=-1, keepdims=True) * v
    y = y + bonus.reshape(b_, s_, D_RWKV)
    return y.astype(ps.dtype), g


def chunk_band_attention(q, k, v, q_g, k_g, rel_bias):
    b_, s_, _ = q.shape
    nc = s_ // CHUNK
    pad = N_LEFT_CHUNKS * CHUNK
    def heads(t):
        return t.reshape(b_, s_, N_ATT_HEADS, HEAD_DIM).transpose(0, 2, 1, 3)
    q = heads(rms_norm(q.reshape(b_, s_, N_ATT_HEADS, HEAD_DIM), q_g).reshape(b_, s_, D_ATT))
    k = heads(rms_norm(k.reshape(b_, s_, N_ATT_HEADS, HEAD_DIM), k_g).reshape(b_, s_, D_ATT))
    v = heads(v)
    qc = q.reshape(b_, N_ATT_HEADS, nc, CHUNK, HEAD_DIM)
    def band(t):
        tp = jnp.pad(t, ((0, 0), (0, 0), (pad, 0), (0, 0))).reshape(b_, N_ATT_HEADS, nc + N_LEFT_CHUNKS, CHUNK, HEAD_DIM)
        return jnp.concatenate([tp[:, :, i:i + nc] for i in range(N_LEFT_CHUNKS + 1)], axis=3)
    kb, vb = band(k), band(v)
    s = jnp.einsum('bhnqd,bhnkd->bhnqk', qc, kb).astype(jnp.float32) * (HEAD_DIM ** -0.5)
    dist = jnp.arange(CHUNK)[:, None] + pad - jnp.arange(BAND)[None, :]
    rel_idx = jnp.clip(dist, -(CHUNK - 1), REL_CLIP) + (CHUNK - 1)
    bias = rel_bias[:, rel_idx].astype(jnp.float32)
    valid = (jnp.arange(nc)[:, None] * CHUNK + jnp.arange(BAND)[None, :] - pad) >= 0
    s = s + bias[None, :, None]
    s = jnp.where(valid[None, None, :, None, :], s, -1e30)
    p = jax.nn.softmax(s, axis=-1).astype(v.dtype)
    o = jnp.einsum('bhnqk,bhnkd->bhnqd', p, vb)
    return o.reshape(b_, N_ATT_HEADS, s_, HEAD_DIM).transpose(0, 2, 1, 3).reshape(b_, s_, D_ATT)


def setup_inputs(seed: int = 0) -> dict:
    key = jax.random.key(seed)
    ks = jax.random.split(key, 22)
    n = lambda i, shape: jax.random.normal(ks[i], shape, jnp.float32)
    L = DEPTH
    return {
        'x': n(0, (BATCH, SEQ, D_MODEL)),
        'c': n(1, (BATCH, D_MODEL)),
        'norm_g': 1.0 + 0.1 * n(2, (L, D_MODEL)),
        'w_ada': n(3, (L, D_MODEL, 3 * D_MODEL)) * (0.5 * D_MODEL ** -0.5),
        'b_ada': 0.02 * n(4, (L, 3 * D_MODEL)),
        'w_in': n(5, (L, D_MODEL, D_IN)) * (D_MODEL ** -0.5),
        'mu_shift': jax.random.uniform(ks[6], (L, D_SHIFT), jnp.float32),
        'w0': -1.0 + 0.5 * n(7, (L, D_RWKV)),
        'w2': n(8, (L, LORA_W, D_RWKV)) * (0.5 * LORA_W ** -0.5),
        'a0': 0.5 * n(9, (L, D_RWKV)),
        'a2': n(10, (L, LORA_A, D_RWKV)) * (0.5 * LORA_A ** -0.5),
        'k_k': 0.85 + 0.05 * n(11, (L, D_RWKV)),
        'k_a': 1.0 + 0.05 * n(12, (L, D_RWKV)),
        'r_k': 0.1 * n(13, (L, N_RWKV_HEADS, HEAD_DIM)),
        'lnx_g': 1.0 + 0.1 * n(14, (L, D_RWKV)),
        'lnx_b': 0.02 * n(15, (L, D_RWKV)),
        'q_norm_g': 1.0 + 0.1 * n(16, (L, HEAD_DIM)),
        'k_norm_g': 1.0 + 0.1 * n(17, (L, HEAD_DIM)),
        'rel_bias': 0.5 * n(18, (L, N_ATT_HEADS, N_REL)),
        'w_out': n(19, (L, D_MIX, D_MODEL)) * (D_MIX ** -0.5),
    }


def reference(x, c, norm_g, w_ada, b_ada, w_in, mu_shift, w0, w2, a0, a2, k_k, k_a, r_k, lnx_g, lnx_b, q_norm_g, k_norm_g, rel_bias, w_out):
    c_act = jax.nn.silu(c)
    for l in range(DEPTH):
        mod = c_act @ w_ada[l] + b_ada[l]
        shift, scale, gate = jnp.split(mod, 3, axis=-1)
        h = rms_norm(x, norm_g[l]) * (1.0 + scale[:, None, :]) + shift[:, None, :]
        proj = h @ w_in[l]
        ps = token_shift(proj[..., :D_SHIFT], mu_shift[l])
        pa = proj[..., D_SHIFT:]
        y_r, g_r = rwkv7_mix(ps, w0[l], w2[l], a0[l], a2[l], k_k[l], k_a[l], r_k[l], lnx_g[l], lnx_b[l])
        q, ka, va, g_a = jnp.split(pa, 4, axis=-1)
        y_a = chunk_band_attention(q, ka, va, q_norm_g[l], k_norm_g[l], rel_bias[l])
        y = jnp.concatenate([y_r * jax.nn.silu(g_r), y_a * jax.nn.silu(g_a)], axis=-1)
        x = x + gate[:, None, :] * (y @ w_out[l])
    return x
```

```python
import functools

import jax
import jax.numpy as jnp
from jax import lax
from jax.experimental import pallas as pl
from jax.experimental.pallas import tpu as pltpu

D_MODEL = 1024
CHUNK = 64
HEAD_DIM = 64
D_RWKV = 512
D_ATT = 512
N_HEADS = 8
LORA = 64
N_LEFT = 8
PAD = N_LEFT * CHUNK
BAND = PAD + CHUNK
WPAD = PAD + CHUNK
WBAND = WPAD + CHUNK
REL_CLIP = 128
N_REL = CHUNK + REL_CLIP
RMS_EPS = 1e-6
GN_EPS = 64e-5
D_SHIFT = 4 * D_RWKV + 2 * LORA
D_IN = D_SHIFT + 4 * D_ATT
PAIR = 2 * HEAD_DIM
N_PAIRS = N_HEADS // 2

VMEM_LIMIT_V7X = 56 * 1024 * 1024

F32 = jnp.float32
BF16 = jnp.bfloat16


def _dot(a, b):
    return jnp.dot(a, b, preferred_element_type=F32)


def _dot_nt(a, b):
    return lax.dot_general(a, b, (((1,), (1,)), ((), ())), preferred_element_type=F32)


def _split2(x):
    hi = x.astype(BF16)
    lo = (x - hi.astype(F32)).astype(BF16)
    return hi, lo


def _split3(x):
    hi = x.astype(BF16)
    r1 = x - hi.astype(F32)
    mid = r1.astype(BF16)
    lo = (r1 - mid.astype(F32)).astype(BF16)
    return hi, mid, lo


def _params(sem):
    return pltpu.CompilerParams(dimension_semantics=sem, vmem_limit_bytes=VMEM_LIMIT_V7X)


def _ada_kernel(c_ref, w_ref, b_ref, o_ref):
    c = c_ref[...]
    ca = c / (1.0 + jnp.exp(-c))
    o_ref[0] = _dot(ca.astype(BF16), w_ref[0].astype(BF16)) + b_ref[0]


def _ada_mod(c, w_ada, b_ada):
    n_layers, d, d3 = w_ada.shape
    b = c.shape[0]
    tn = 1024
    return pl.pallas_call(
        _ada_kernel,
        grid=(n_layers, d3 // tn),
        in_specs=[
            pl.BlockSpec((b, d), lambda l, j: (0, 0)),
            pl.BlockSpec((1, d, tn), lambda l, j: (l, 0, j)),
            pl.BlockSpec((1, 1, tn), lambda l, j: (l, 0, j)),
        ],
        out_specs=pl.BlockSpec((1, b, tn), lambda l, j: (l, 0, j)),
        out_shape=jax.ShapeDtypeStruct((n_layers, b, d3), F32),
        compiler_params=_params(("arbitrary", "arbitrary")),
        name="ada_mod",
    )(c, w_ada, b_ada.reshape(n_layers, 1, d3))


def _bias_kernel(tbl_ref, o_ref):
    l = pl.program_id(0)
    h = pl.program_id(1)
    qi = lax.broadcasted_iota(jnp.int32, (CHUNK, WBAND), 0)
    ji = lax.broadcasted_iota(jnp.int32, (CHUNK, WBAND), 1)
    idx = jnp.clip(qi + WPAD - ji, -(CHUNK - 1), REL_CLIP) + (CHUNK - 1)
    base = (l * N_HEADS + h) * N_REL

    def body(m, acc):
        return jnp.where(idx == m, tbl_ref[base + m], acc)

    o_ref[0, 0] = lax.fori_loop(0, N_REL, body, jnp.zeros((CHUNK, WBAND), F32))


def _bias_tables(rel_bias):
    n_layers = rel_bias.shape[0]
    return pl.pallas_call(
        _bias_kernel,
        grid=(n_layers, N_HEADS),
        in_specs=[pl.BlockSpec(memory_space=pltpu.SMEM)],
        out_specs=pl.BlockSpec((1, 1, CHUNK, WBAND), lambda l, h: (l, h, 0, 0)),
        out_shape=jax.ShapeDtypeStruct((n_layers, N_HEADS, CHUNK, WBAND), F32),
        compiler_params=_params(("arbitrary", "arbitrary")),
        name="bias_tables",
    )(rel_bias.reshape(-1))


def _inproj_kernel(x_ref, mod_ref, g_ref, w_ref, mu_ref, rkvg_ref, wa_ref, att_ref, carry_ref):
    i = pl.program_id(1)
    tm = x_ref.shape[1]

    @pl.when(i == 0)
    def _():
        carry_ref[...] = jnp.zeros_like(carry_ref)

    x = x_ref[0]
    y = x * lax.rsqrt(jnp.mean(x * x, axis=-1, keepdims=True) + RMS_EPS)
    shift = mod_ref[0, :, 0:D_MODEL]
    scale = mod_ref[0, :, D_MODEL:2 * D_MODEL]
    h = ((y * g_ref[...]) * (1.0 + scale) + shift).astype(BF16)

    p = _dot(h, w_ref[:, 0:D_SHIFT])
    row = lax.broadcasted_iota(jnp.int32, (tm, D_SHIFT), 0)
    prev = jnp.where(row == 0, carry_ref[...], pltpu.roll(p, 1, 0))
    carry_ref[...] = p[tm - 1:tm, :]
    ps = p + mu_ref[...] * (prev - p)
    rkvg_ref[0] = ps[:, 0:4 * D_RWKV].astype(BF16)
    wa_ref[0] = ps[:, 4 * D_RWKV:D_SHIFT].astype(BF16)
    att_ref[0] = _dot(h, w_ref[:, D_SHIFT:D_IN]).astype(BF16)


def _inproj(x, mod_l, norm_g_l, w_in_l, mu_l, tm):
    b, s, d = x.shape
    return pl.pallas_call(
        _inproj_kernel,
        grid=(b, s // tm),
        in_specs=[
            pl.BlockSpec((1, tm, d), lambda bi, i: (bi, i, 0)),
            pl.BlockSpec((1, 1, 3 * d), lambda bi, i: (bi, 0, 0)),
            pl.BlockSpec((1, d), lambda bi, i: (0, 0)),
            pl.BlockSpec((d, D_IN), lambda bi, i: (0, 0)),
            pl.BlockSpec((1, D_SHIFT), lambda bi, i: (0, 0)),
        ],
        out_specs=[
            pl.BlockSpec((1, tm, 4 * D_RWKV), lambda bi, i: (bi, i, 0)),
            pl.BlockSpec((1, tm, 2 * LORA), lambda bi, i: (bi, i, 0)),
            pl.BlockSpec((1, tm, 4 * D_ATT), lambda bi, i: (bi, i, 0)),
        ],
        out_shape=[
            jax.ShapeDtypeStruct((b, s, 4 * D_RWKV), BF16),
            jax.ShapeDtypeStruct((b, s, 2 * LORA), BF16),
            jax.ShapeDtypeStruct((b, s, 4 * D_ATT), BF16),
        ],
        scratch_shapes=[pltpu.VMEM((1, D_SHIFT), F32)],
        compiler_params=_params(("arbitrary", "arbitrary")),
        name="inproj",
    )(x, mod_l.reshape(b, 1, 3 * d), norm_g_l.reshape(1, d), w_in_l, mu_l.reshape(1, D_SHIFT))


def _stack2(x, lane_lo):
    zero = jnp.zeros_like(x)
    return jnp.concatenate([jnp.where(lane_lo, x, zero), jnp.where(lane_lo, zero, x)], axis=0)


def _rwkv_kernel(rkvg_ref, wa_ref, w2a2_ref, vec_ref, ones_ref, tri_ref, o_ref,
                 st_ref, lw_s, al_s, be_s, km_s, y_s):
    i = pl.program_id(1)
    ts = rkvg_ref.shape[1]
    n_chunks = ts // CHUNK

    @pl.when(i == 0)
    def _():
        st_ref[...] = jnp.zeros_like(st_ref)

    w0 = vec_ref[0:1, :]
    a0 = vec_ref[1:2, :]
    k_k = vec_ref[2:3, :]
    k_a = vec_ref[3:4, :]
    r_k = vec_ref[4:5, :]
    lnx_g = vec_ref[5:6, :]
    lnx_b = vec_ref[6:7, :]
    ones = ones_ref[...]

    def group_sum(x):
        hi, lo = _split2(x)
        parts = []
        for p in range(N_PAIRS):
            sl = slice(p * PAIR, (p + 1) * PAIR)
            parts.append(_dot(hi[:, sl], ones) + _dot(lo[:, sl], ones))
        return jnp.concatenate(parts, axis=1)

    k = rkvg_ref[0, :, D_RWKV:2 * D_RWKV].astype(F32)
    wa = wa_ref[0].astype(F32)
    lane_wa = lax.broadcasted_iota(jnp.int32, wa.shape, 1)
    tw = jnp.where(lane_wa < LORA, jnp.tanh(wa), wa).astype(BF16)
    za = _dot(tw, w2a2_ref[...])
    z = w0 + za[:, 0:D_RWKV]
    aa = a0 + za[:, D_RWKV:2 * D_RWKV]
    w_log = -(jnp.maximum(-z, 0.0) + jnp.log(1.0 + jnp.exp(-jnp.abs(z)))) - 0.5
    lw_s[...] = -jnp.exp(w_log)
    a = 1.0 / (1.0 + jnp.exp(-aa))
    kk = k * k_k
    kk = kk * lax.rsqrt(jnp.maximum(group_sum(kk * kk), 1e-24))
    al_s[...] = -kk
    be_s[...] = kk * a
    km_s[...] = k * (1.0 + (a - 1.0) * k_a)

    row = lax.broadcasted_iota(jnp.int32, (CHUNK, PAIR), 0)
    lane = lax.broadcasted_iota(jnp.int32, (CHUNK, PAIR), 1)
    lane_lo = lane < HEAD_DIM
    col = jnp.where(lane_lo, lane, lane - HEAD_DIM)
    m_strict = col < row
    m_incl = col <= row
    eye = (col == row).astype(F32)
    row2 = lax.broadcasted_iota(jnp.int32, (PAIR, PAIR), 0)
    lane2 = lax.broadcasted_iota(jnp.int32, (PAIR, PAIR), 1)
    same_head = (row2 < HEAD_DIM) == (lane2 < HEAD_DIM)
    diag2 = row2 == lane2
    lane_lo2 = lane_lo
    tri = tri_ref[...]

    def chunk_body(c, carry):
        r0 = pl.multiple_of(c * CHUNK, CHUNK)
        rows = pl.ds(r0, CHUNK)
        r = rkvg_ref[0, rows, 0:D_RWKV].astype(F32)
        v = rkvg_ref[0, rows, 2 * D_RWKV:3 * D_RWKV].astype(F32)
        lw = lw_s[rows, :]
        al = al_s[rows, :]
        be = be_s[rows, :]
        km = km_s[rows, :]
        hi, mid, lo = _split3(lw)
        cum = _dot(tri, hi) + _dot(tri, mid) + _dot(tri, lo)
        last = jnp.sum(lw, axis=0, keepdims=True)
        w_in = jnp.exp(cum)
        w_prev = jnp.exp(cum - lw)
        w_inv = jnp.exp(-cum)
        w_end = jnp.exp(last - cum)
        rt = r * w_in
        at = al * w_prev
        bt = be * w_inv
        kt = km * w_inv
        bw = be * w_end
        kw = km * w_end
        ys = []
        for p in range(N_PAIRS):
            sl = slice(p * PAIR, (p + 1) * PAIR)
            at_p = at[:, sl].astype(BF16)
            rt_p = rt[:, sl]
            v_p = v[:, sl].astype(BF16)
            lhs = jnp.concatenate([at_p, rt_p.astype(BF16)], axis=0)
            rhs = jnp.concatenate([_stack2(bt[:, sl].astype(BF16), lane_lo2),
                                   _stack2(kt[:, sl].astype(BF16), lane_lo2)], axis=0)
            amat = _dot_nt(lhs, rhs)
            n1 = jnp.where(m_strict, amat[0:CHUNK, 0:PAIR], 0.0)
            a_ak = jnp.where(m_strict, amat[0:CHUNK, PAIR:2 * PAIR], 0.0)
            a_rb = jnp.where(m_incl, amat[CHUNK:2 * CHUNK, 0:PAIR], 0.0)
            a_rk = jnp.where(m_incl, amat[CHUNK:2 * CHUNK, PAIR:2 * PAIR], 0.0)
            n1b = n1.astype(BF16)
            npow = _dot(n1b, _stack2(n1b, lane_lo2))
            tmat = eye + n1
            for _ in range(4):
                npb = npow.astype(BF16)
                both = _dot(jnp.concatenate([tmat.astype(BF16), npb], axis=0), _stack2(npb, lane_lo2))
                tmat = tmat + both[0:CHUNK]
                npow = both[CHUNK:2 * CHUNK]
            tmat = tmat + _dot(tmat.astype(BF16), _stack2(npow.astype(BF16), lane_lo2))
            tb = tmat.astype(BF16)
            av = _dot(jnp.concatenate([a_ak.astype(BF16), a_rk.astype(BF16)], axis=0),
                      _stack2(v_p, lane_lo2))
            akv = av[0:CHUNK]
            arkv = av[CHUNK:2 * CHUNK]
            t2 = _dot(tb, jnp.concatenate([_stack2(at_p, lane_lo2),
                                           _stack2(akv.astype(BF16), lane_lo2)], axis=1))
            a_pr = t2[:, 0:PAIR].astype(BF16)
            uv = t2[:, PAIR:2 * PAIR].astype(BF16)
            r2 = _dot(a_rb.astype(BF16), jnp.concatenate([_stack2(a_pr, lane_lo2),
                                                          _stack2(uv, lane_lo2)], axis=1))
            r_pr = rt_p + r2[:, 0:PAIR]
            y0 = arkv + r2[:, PAIR:2 * PAIR]
            lt = jnp.concatenate([bw[:, sl], kw[:, sl]], axis=0).T.astype(BF16)
            rgt = jnp.concatenate([
                jnp.concatenate([a_pr, uv], axis=1),
                jnp.concatenate([jnp.zeros_like(v_p), v_p], axis=1)], axis=0)
            gh = _dot(lt, rgt)
            w_all = jnp.exp(jnp.sum(lw[:, sl], axis=0, keepdims=True))
            g = jnp.where(same_head, gh[:, 0:PAIR], 0.0) + jnp.where(diag2, w_all, 0.0)
            hmat = jnp.where(same_head, gh[:, PAIR:2 * PAIR], 0.0)
            st = st_ref[p]
            seq = _dot(jnp.concatenate([r_pr.astype(BF16), g.astype(BF16)], axis=0), st.astype(BF16))
            ys.append(seq[0:CHUNK] + y0)
            st_ref[p] = seq[CHUNK:CHUNK + PAIR] + hmat
        y_s[rows, :] = jnp.concatenate(ys, axis=1)
        return carry

    lax.fori_loop(0, n_chunks, chunk_body, 0)

    y = y_s[...]
    r = rkvg_ref[0, :, 0:D_RWKV].astype(F32)
    v = rkvg_ref[0, :, 2 * D_RWKV:3 * D_RWKV].astype(F32)
    g = rkvg_ref[0, :, 3 * D_RWKV:4 * D_RWKV].astype(F32)
    inv_n = 1.0 / HEAD_DIM
    yc = y - group_sum(y) * inv_n
    yn = yc * lax.rsqrt(group_sum(yc * yc) * inv_n + GN_EPS)
    out = yn * lnx_g + lnx_b + group_sum(r * km_s[...] * r_k) * v
    o_ref[0] = (out * (g / (1.0 + jnp.exp(-g)))).astype(BF16)


def _rwkv(rkvg, wa, w2a2, vecs, ts):
    b, s, _ = rkvg.shape
    ones = jnp.asarray(
        (jnp.arange(PAIR)[:, None] // HEAD_DIM) == (jnp.arange(PAIR)[None, :] // HEAD_DIM), BF16)
    tri = jnp.asarray(jnp.arange(CHUNK)[:, None] >= jnp.arange(CHUNK)[None, :], BF16)
    return pl.pallas_call(
        _rwkv_kernel,
        grid=(b, s // ts),
        in_specs=[
            pl.BlockSpec((1, ts, 4 * D_RWKV), lambda bi, i: (bi, i, 0)),
            pl.BlockSpec((1, ts, 2 * LORA), lambda bi, i: (bi, i, 0)),
            pl.BlockSpec((2 * LORA, 2 * D_RWKV), lambda bi, i: (0, 0)),
            pl.BlockSpec((8, D_RWKV), lambda bi, i: (0, 0)),
            pl.BlockSpec((PAIR, PAIR), lambda bi, i: (0, 0)),
            pl.BlockSpec((CHUNK, CHUNK), lambda bi, i: (0, 0)),
        ],
        out_specs=pl.BlockSpec((1, ts, D_RWKV), lambda bi, i: (bi, i, 0)),
        out_shape=jax.ShapeDtypeStruct((b, s, D_RWKV), BF16),
        scratch_shapes=[
            pltpu.VMEM((N_PAIRS, PAIR, PAIR), F32),
            pltpu.VMEM((ts, D_RWKV), F32),
            pltpu.VMEM((ts, D_RWKV), F32),
            pltpu.VMEM((ts, D_RWKV), F32),
            pltpu.VMEM((ts, D_RWKV), F32),
            pltpu.VMEM((ts, D_RWKV), F32),
        ],
        compiler_params=_params(("arbitrary", "arbitrary")),
        name="rwkv7",
    )(rkvg, wa, w2a2, vecs, ones, tri)


def _attn_kernel(q_ref, k_ref, v_ref, g_ref, qg_ref, kg_ref, ones_ref, bias_ref, o_ref,
                 kp_s, vp_s, qn_s):
    i = pl.program_id(1)
    tq = q_ref.shape[1]
    s = k_ref.shape[1]
    n_chunks = tq // CHUNK
    ones = ones_ref[...]

    def head_norm(x, gain):
        hi, lo = _split2(x * x)
        parts = []
        for p in range(N_PAIRS):
            sl = slice(p * PAIR, (p + 1) * PAIR)
            parts.append(_dot(hi[:, sl], ones) + _dot(lo[:, sl], ones))
        ms = jnp.concatenate(parts, axis=1) * (1.0 / HEAD_DIM)
        return (x * lax.rsqrt(ms + RMS_EPS) * gain).astype(BF16)

    @pl.when(i == 0)
    def _():
        kp_s[0:WPAD, :] = jnp.zeros((WPAD, D_ATT), BF16)
        vp_s[0:WPAD, :] = jnp.zeros((WPAD, D_ATT), BF16)
        kp_s[WPAD:WPAD + s, :] = head_norm(k_ref[0].astype(F32), kg_ref[...])
        vp_s[WPAD:WPAD + s, :] = v_ref[0]

    lane = lax.broadcasted_iota(jnp.int32, (CHUNK, PAIR), 1)
    lane_lo = lane < HEAD_DIM
    jpos = lax.broadcasted_iota(jnp.int32, (CHUNK, WBAND), 1)
    qn_s[...] = head_norm(q_ref[0].astype(F32), qg_ref[...])

    def chunk_body(c, carry):
        r0 = pl.multiple_of(c * CHUNK, CHUNK)
        q0 = i * tq + r0
        kb = kp_s[pl.ds(pl.multiple_of(q0, CHUNK), WBAND), :]
        vb = vp_s[pl.ds(pl.multiple_of(q0, CHUNK), WBAND), :]
        valid = jpos >= jnp.maximum(WPAD - q0, WPAD - PAD)
        qc = qn_s[pl.ds(r0, CHUNK), :]
        outs = []
        for p in range(N_PAIRS):
            sl = slice(p * PAIR, (p + 1) * PAIR)
            q_p = qc[:, sl]
            sc = _dot_nt(_stack2(q_p, lane_lo), kb[:, sl]) * (HEAD_DIM ** -0.5)
            probs = []
            dens = []
            for e in range(2):
                se = sc[e * CHUNK:(e + 1) * CHUNK] + bias_ref[0, 2 * p + e]
                se = jnp.where(valid, se, -1e30)
                pe = jnp.exp(se - jnp.max(se, axis=-1, keepdims=True))
                dens.append(jnp.sum(pe, axis=-1, keepdims=True))
                probs.append(pe.astype(BF16))
            pv = _dot(jnp.concatenate(probs, axis=0), vb[:, sl])
            outs.append(jnp.where(lane_lo, pv[0:CHUNK] / dens[0], pv[CHUNK:2 * CHUNK] / dens[1]))
        o = jnp.concatenate(outs, axis=1)
        g = g_ref[0, pl.ds(r0, CHUNK), :].astype(F32)
        o_ref[0, pl.ds(r0, CHUNK), :] = (o * (g / (1.0 + jnp.exp(-g)))).astype(BF16)
        return carry

    lax.fori_loop(0, n_chunks, chunk_body, 0)


def _attn(att, q_g, k_g, bias_l, tq):
    b, s, _ = att.shape
    ones = jnp.asarray(
        (jnp.arange(PAIR)[:, None] // HEAD_DIM) == (jnp.arange(PAIR)[None, :] // HEAD_DIM), BF16)
    qg = jnp.tile(q_g, N_HEADS).reshape(1, D_ATT)
    kg = jnp.tile(k_g, N_HEADS).reshape(1, D_ATT)
    return pl.pallas_call(
        _attn_kernel,
        grid=(b, s // tq),
        in_specs=[
            pl.BlockSpec((1, tq, D_ATT), lambda bi, i: (bi, i, 0)),
            pl.BlockSpec((1, s, D_ATT), lambda bi, i: (bi, 0, 1)),
            pl.BlockSpec((1, s, D_ATT), lambda bi, i: (bi, 0, 2)),
            pl.BlockSpec((1, tq, D_ATT), lambda bi, i: (bi, i, 3)),
            pl.BlockSpec((1, D_ATT), lambda bi, i: (0, 0)),
            pl.BlockSpec((1, D_ATT), lambda bi, i: (0, 0)),
            pl.BlockSpec((PAIR, PAIR), lambda bi, i: (0, 0)),
            pl.BlockSpec((1, N_HEADS, CHUNK, WBAND), lambda bi, i: (0, 0, 0, 0)),
        ],
        out_specs=pl.BlockSpec((1, tq, D_ATT), lambda bi, i: (bi, i, 0)),
        out_shape=jax.ShapeDtypeStruct((b, s, D_ATT), BF16),
        scratch_shapes=[
            pltpu.VMEM((WPAD + s, D_ATT), BF16),
            pltpu.VMEM((WPAD + s, D_ATT), BF16),
            pltpu.VMEM((tq, D_ATT), BF16),
        ],
        compiler_params=_params(("arbitrary", "arbitrary")),
        name="band_attn",
    )(att, att, att, att, qg, kg, ones, bias_l)


def _outproj_kernel(x_ref, yr_ref, ya_ref, mod_ref, w_ref, o_ref):
    acc = _dot(yr_ref[0], w_ref[0:D_RWKV, :]) + _dot(ya_ref[0], w_ref[D_RWKV:D_RWKV + D_ATT, :])
    gate = mod_ref[0, :, 2 * D_MODEL:3 * D_MODEL]
    o_ref[0] = x_ref[0] + gate * acc


def _outproj(x, yr, ya, mod_l, w_out_l, tm):
    b, s, d = x.shape
    return pl.pallas_call(
        _outproj_kernel,
        grid=(b, s // tm),
        in_specs=[
            pl.BlockSpec((1, tm, d), lambda bi, i: (bi, i, 0)),
            pl.BlockSpec((1, tm, D_RWKV), lambda bi, i: (bi, i, 0)),
            pl.BlockSpec((1, tm, D_ATT), lambda bi, i: (bi, i, 0)),
            pl.BlockSpec((1, 1, 3 * d), lambda bi, i: (bi, 0, 0)),
            pl.BlockSpec((D_RWKV + D_ATT, d), lambda bi, i: (0, 0)),
        ],
        out_specs=pl.BlockSpec((1, tm, d), lambda bi, i: (bi, i, 0)),
        out_shape=jax.ShapeDtypeStruct((b, s, d), F32),
        compiler_params=_params(("arbitrary", "arbitrary")),
        name="outproj",
    )(x, yr, ya, mod_l.reshape(b, 1, 3 * d), w_out_l)


def _tile(s, want):
    t = min(want, s)
    assert s % t == 0 and t % CHUNK == 0
    return t


def kernel(x, c, norm_g, w_ada, b_ada, w_in, mu_shift, w0, w2, a0, a2, k_k, k_a, r_k, lnx_g, lnx_b,
           q_norm_g, k_norm_g, rel_bias, w_out):
    n_layers = w_in.shape[0]
    s = x.shape[1]
    tm = _tile(s, 512)
    ts = _tile(s, 256)
    tq = _tile(s, 256)
    mod = _ada_mod(c, w_ada, b_ada)
    bias = _bias_tables(rel_bias)
    w_in_b = w_in.astype(BF16)
    w_out_b = w_out.astype(BF16)
    zeros = jnp.zeros((LORA, D_RWKV), F32)
    for l in range(n_layers):
        rkvg, wa, att = _inproj(x, mod[l], norm_g[l], w_in_b[l], mu_shift[l], tm)
        w2a2 = jnp.concatenate([jnp.concatenate([w2[l], zeros], axis=1),
                                jnp.concatenate([zeros, a2[l]], axis=1)], axis=0).astype(BF16)
        vecs = jnp.stack([w0[l], a0[l], k_k[l], k_a[l], r_k[l].reshape(-1), lnx_g[l], lnx_b[l],
                          jnp.zeros((D_RWKV,), F32)])
        yr = _rwkv(rkvg, wa, w2a2, vecs, ts)
        ya = _attn(att, q_norm_g[l], k_norm_g[l], bias[l:l + 1], tq)
        x = _outproj(x, yr, ya, mod[l], w_out_b[l], tm)
    return x
```

```python
import functools

import jax
import jax.numpy as jnp
from jax import lax
from jax.experimental import pallas as pl
from jax.experimental.pallas import tpu as pltpu

D_MODEL = 1024
CHUNK = 64
HEAD_DIM = 64
D_RWKV = 512
D_ATT = 512
N_HEADS = 8
LORA = 64
N_LEFT = 8
PAD = N_LEFT * CHUNK
BAND = PAD + CHUNK
WPAD = PAD + CHUNK
WBAND = WPAD + CHUNK
REL_CLIP = 128
N_REL = CHUNK + REL_CLIP
RMS_EPS = 1e-6
GN_EPS = 64e-5
D_SHIFT = 4 * D_RWKV + 2 * LORA
D_IN = D_SHIFT + 4 * D_ATT
PAIR = 2 * HEAD_DIM
N_PAIRS = N_HEADS // 2
RWKV_CHUNKS_PER_STEP = 4

VMEM_LIMIT_V7X = 56 * 1024 * 1024

F32 = jnp.float32
BF16 = jnp.bfloat16


def _dot(a, b):
    return jnp.dot(a, b, preferred_element_type=F32)


def _dot_nt(a, b):
    return lax.dot_general(a, b, (((1,), (1,)), ((), ())), preferred_element_type=F32)


def _split2(x):
    hi = x.astype(BF16)
    lo = (x - hi.astype(F32)).astype(BF16)
    return hi, lo


def _split3(x):
    hi = x.astype(BF16)
    r1 = x - hi.astype(F32)
    mid = r1.astype(BF16)
    lo = (r1 - mid.astype(F32)).astype(BF16)
    return hi, mid, lo


def _params(sem):
    return pltpu.CompilerParams(dimension_semantics=sem, vmem_limit_bytes=VMEM_LIMIT_V7X)


def _ada_kernel(c_ref, w_ref, b_ref, o_ref):
    c = c_ref[...]
    ca = c / (1.0 + jnp.exp(-c))
    o_ref[0] = _dot(ca.astype(BF16), w_ref[0].astype(BF16)) + b_ref[0]


def _ada_mod(c, w_ada, b_ada):
    n_layers, d, d3 = w_ada.shape
    b = c.shape[0]
    tn = 1024
    return pl.pallas_call(
        _ada_kernel,
        grid=(n_layers, d3 // tn),
        in_specs=[
            pl.BlockSpec((b, d), lambda l, j: (0, 0)),
            pl.BlockSpec((1, d, tn), lambda l, j: (l, 0, j)),
            pl.BlockSpec((1, 1, tn), lambda l, j: (l, 0, j)),
        ],
        out_specs=pl.BlockSpec((1, b, tn), lambda l, j: (l, 0, j)),
        out_shape=jax.ShapeDtypeStruct((n_layers, b, d3), F32),
        compiler_params=_params(("arbitrary", "arbitrary")),
        name="ada_mod",
    )(c, w_ada, b_ada.reshape(n_layers, 1, d3))


def _bias_kernel(tbl_ref, o_ref):
    l = pl.program_id(0)
    h = pl.program_id(1)
    qi = lax.broadcasted_iota(jnp.int32, (CHUNK, WBAND), 0)
    ji = lax.broadcasted_iota(jnp.int32, (CHUNK, WBAND), 1)
    idx = jnp.clip(qi + WPAD - ji, -(CHUNK - 1), REL_CLIP) + (CHUNK - 1)
    base = (l * N_HEADS + h) * N_REL

    def body(m, acc):
        return jnp.where(idx == m, tbl_ref[base + m], acc)

    o_ref[0, 0] = lax.fori_loop(0, N_REL, body, jnp.zeros((CHUNK, WBAND), F32))


def _bias_tables(rel_bias):
    n_layers = rel_bias.shape[0]
    return pl.pallas_call(
        _bias_kernel,
        grid=(n_layers, N_HEADS),
        in_specs=[pl.BlockSpec(memory_space=pltpu.SMEM)],
        out_specs=pl.BlockSpec((1, 1, CHUNK, WBAND), lambda l, h: (l, h, 0, 0)),
        out_shape=jax.ShapeDtypeStruct((n_layers, N_HEADS, CHUNK, WBAND), F32),
        compiler_params=_params(("arbitrary", "arbitrary")),
        name="bias_tables",
    )(rel_bias.reshape(-1))


def _inproj_kernel(x_ref, mod_ref, g_ref, w_ref, mu_ref, rkvg_ref, wa_ref, att_ref, carry_ref):
    i = pl.program_id(1)
    tm = x_ref.shape[1]

    @pl.when(i == 0)
    def _():
        carry_ref[...] = jnp.zeros_like(carry_ref)

    x = x_ref[0]
    y = x * lax.rsqrt(jnp.mean(x * x, axis=-1, keepdims=True) + RMS_EPS)
    shift = mod_ref[0, :, 0:D_MODEL]
    scale = mod_ref[0, :, D_MODEL:2 * D_MODEL]
    h = ((y * g_ref[...]) * (1.0 + scale) + shift).astype(BF16)

    p = _dot(h, w_ref[:, 0:D_SHIFT])
    row = lax.broadcasted_iota(jnp.int32, (tm, D_SHIFT), 0)
    prev = jnp.where(row == 0, carry_ref[...], pltpu.roll(p, 1, 0))
    carry_ref[...] = p[tm - 1:tm, :]
    ps = p + mu_ref[...] * (prev - p)
    rkvg_ref[0] = ps[:, 0:4 * D_RWKV].astype(BF16)
    wa_ref[0] = ps[:, 4 * D_RWKV:D_SHIFT].astype(BF16)
    att_ref[0] = _dot(h, w_ref[:, D_SHIFT:D_IN]).astype(BF16)


def _inproj(x, mod_l, norm_g_l, w_in_l, mu_l, tm):
    b, s, d = x.shape
    return pl.pallas_call(
        _inproj_kernel,
        grid=(b, s // tm),
        in_specs=[
            pl.BlockSpec((1, tm, d), lambda bi, i: (bi, i, 0)),
            pl.BlockSpec((1, 1, 3 * d), lambda bi, i: (bi, 0, 0)),
            pl.BlockSpec((1, d), lambda bi, i: (0, 0)),
            pl.BlockSpec((d, D_IN), lambda bi, i: (0, 0)),
            pl.BlockSpec((1, D_SHIFT), lambda bi, i: (0, 0)),
        ],
        out_specs=[
            pl.BlockSpec((1, tm, 4 * D_RWKV), lambda bi, i: (bi, i, 0)),
            pl.BlockSpec((1, tm, 2 * LORA), lambda bi, i: (bi, i, 0)),
            pl.BlockSpec((1, tm, 4 * D_ATT), lambda bi, i: (bi, i, 0)),
        ],
        out_shape=[
            jax.ShapeDtypeStruct((b, s, 4 * D_RWKV), BF16),
            jax.ShapeDtypeStruct((b, s, 2 * LORA), BF16),
            jax.ShapeDtypeStruct((b, s, 4 * D_ATT), BF16),
        ],
        scratch_shapes=[pltpu.VMEM((1, D_SHIFT), F32)],
        compiler_params=_params(("arbitrary", "arbitrary")),
        name="inproj",
    )(x, mod_l.reshape(b, 1, 3 * d), norm_g_l.reshape(1, d), w_in_l, mu_l.reshape(1, D_SHIFT))


def _stack2(x, lane_lo):
    zero = jnp.zeros_like(x)
    return jnp.concatenate([jnp.where(lane_lo, x, zero), jnp.where(lane_lo, zero, x)], axis=0)


def _rwkv_kernel(rkvg_ref, wa_ref, w2a2_ref, vec_ref, ones_ref, tri_ref, o_ref,
                 st_ref, lw_s, al_s, be_s, km_s, y_s):
    i = pl.program_id(1)
    ts = rkvg_ref.shape[1]
    n_chunks = ts // CHUNK

    @pl.when(i == 0)
    def _():
        st_ref[...] = jnp.zeros_like(st_ref)

    w0 = vec_ref[0:1, :]
    a0 = vec_ref[1:2, :]
    k_k = vec_ref[2:3, :]
    k_a = vec_ref[3:4, :]
    r_k = vec_ref[4:5, :]
    lnx_g = vec_ref[5:6, :]
    lnx_b = vec_ref[6:7, :]
    ones = ones_ref[...]

    def group_sum(x):
        hi, lo = _split2(x)
        parts = []
        for p in range(N_PAIRS):
            sl = slice(p * PAIR, (p + 1) * PAIR)
            parts.append(_dot(hi[:, sl], ones) + _dot(lo[:, sl], ones))
        return jnp.concatenate(parts, axis=1)

    k = rkvg_ref[0, :, D_RWKV:2 * D_RWKV].astype(F32)
    wa = wa_ref[0].astype(F32)
    lane_wa = lax.broadcasted_iota(jnp.int32, wa.shape, 1)
    tw = jnp.where(lane_wa < LORA, jnp.tanh(wa), wa).astype(BF16)
    za = _dot(tw, w2a2_ref[...])
    z = w0 + za[:, 0:D_RWKV]
    aa = a0 + za[:, D_RWKV:2 * D_RWKV]
    w_log = -(jnp.maximum(-z, 0.0) + jnp.log(1.0 + jnp.exp(-jnp.abs(z)))) - 0.5
    lw_s[...] = -jnp.exp(w_log)
    a = 1.0 / (1.0 + jnp.exp(-aa))
    kk = k * k_k
    kk = kk * lax.rsqrt(jnp.maximum(group_sum(kk * kk), 1e-24))
    al_s[...] = -kk
    be_s[...] = kk * a
    km_s[...] = k * (1.0 + (a - 1.0) * k_a)

    row = lax.broadcasted_iota(jnp.int32, (CHUNK, PAIR), 0)
    lane = lax.broadcasted_iota(jnp.int32, (CHUNK, PAIR), 1)
    lane_lo = lane < HEAD_DIM
    col = jnp.where(lane_lo, lane, lane - HEAD_DIM)
    m_strict = col < row
    m_incl = col <= row
    eye = (col == row).astype(F32)
    row2 = lax.broadcasted_iota(jnp.int32, (PAIR, PAIR), 0)
    lane2 = lax.broadcasted_iota(jnp.int32, (PAIR, PAIR), 1)
    same_head = (row2 < HEAD_DIM) == (lane2 < HEAD_DIM)
    diag2 = row2 == lane2
    lane_lo2 = lane_lo
    tri = tri_ref[...]

    s2 = functools.partial(_stack2, lane_lo=lane_lo2)
    cat0 = functools.partial(jnp.concatenate, axis=0)
    cat1 = functools.partial(jnp.concatenate, axis=1)
    group = RWKV_CHUNKS_PER_STEP

    def group_body(gi, carry):
        prep = []
        for ci in range(group):
            r0 = pl.multiple_of((gi * group + ci) * CHUNK, CHUNK)
            rows = pl.ds(r0, CHUNK)
            r = rkvg_ref[0, rows, 0:D_RWKV].astype(F32)
            v = rkvg_ref[0, rows, 2 * D_RWKV:3 * D_RWKV]
            lw = lw_s[rows, :]
            be = be_s[rows, :]
            km = km_s[rows, :]
            hi, mid, lo = _split3(lw)
            cum = _dot(tri, hi) + _dot(tri, mid) + _dot(tri, lo)
            last = jnp.sum(lw, axis=0, keepdims=True)
            w_inv = jnp.exp(-cum)
            w_end = jnp.exp(last - cum)
            prep.append(dict(
                rows=rows, v=v, lw=lw,
                rt=r * jnp.exp(cum),
                at=(al_s[rows, :] * jnp.exp(cum - lw)).astype(BF16),
                bt=(be * w_inv).astype(BF16),
                kt=(km * w_inv).astype(BF16),
                bw=be * w_end,
                kw=km * w_end))
        chains = [(ci, p) for ci in range(group) for p in range(N_PAIRS)]
        sls = [slice(p * PAIR, (p + 1) * PAIR) for _, p in chains]
        at_c = [prep[ci]["at"][:, sl] for (ci, _), sl in zip(chains, sls)]
        rt_c = [prep[ci]["rt"][:, sl] for (ci, _), sl in zip(chains, sls)]
        v_c = [prep[ci]["v"][:, sl] for (ci, _), sl in zip(chains, sls)]
        amat = [_dot_nt(cat0([at_c[j], rt_c[j].astype(BF16)]),
                        cat0([s2(prep[ci]["bt"][:, sls[j]]), s2(prep[ci]["kt"][:, sls[j]])]))
                for j, (ci, _) in enumerate(chains)]
        n1 = [jnp.where(m_strict, a[0:CHUNK, 0:PAIR], 0.0) for a in amat]
        a_ak = [jnp.where(m_strict, a[0:CHUNK, PAIR:2 * PAIR], 0.0).astype(BF16) for a in amat]
        a_rb = [jnp.where(m_incl, a[CHUNK:2 * CHUNK, 0:PAIR], 0.0).astype(BF16) for a in amat]
        a_rk = [jnp.where(m_incl, a[CHUNK:2 * CHUNK, PAIR:2 * PAIR], 0.0).astype(BF16) for a in amat]
        n1b = [n.astype(BF16) for n in n1]
        npow = [_dot(n, s2(n)) for n in n1b]
        av = [_dot(cat0([a_ak[j], a_rk[j]]), s2(v_c[j])) for j in range(len(chains))]
        tmat = [eye + n for n in n1]
        for _ in range(4):
            npb = [n.astype(BF16) for n in npow]
            both = [_dot(cat0([t.astype(BF16), n]), s2(n)) for t, n in zip(tmat, npb)]
            tmat = [t + b[0:CHUNK] for t, b in zip(tmat, both)]
            npow = [b[CHUNK:2 * CHUNK] for b in both]
        tmat = [t + _dot(t.astype(BF16), s2(n.astype(BF16))) for t, n in zip(tmat, npow)]
        t2 = [_dot(tmat[j].astype(BF16), cat1([s2(at_c[j]), s2(av[j][0:CHUNK].astype(BF16))]))
              for j in range(len(chains))]
        a_pr = [t[:, 0:PAIR].astype(BF16) for t in t2]
        uv = [t[:, PAIR:2 * PAIR].astype(BF16) for t in t2]
        r2 = [_dot(a_rb[j], cat1([s2(a_pr[j]), s2(uv[j])])) for j in range(len(chains))]
        r_pr = [(rt_c[j] + r2[j][:, 0:PAIR]).astype(BF16) for j in range(len(chains))]
        y0 = [av[j][CHUNK:2 * CHUNK] + r2[j][:, PAIR:2 * PAIR] for j in range(len(chains))]
        gmat, hmat = [], []
        for j, (ci, _) in enumerate(chains):
            lt = cat0([prep[ci]["bw"][:, sls[j]], prep[ci]["kw"][:, sls[j]]]).T.astype(BF16)
            rgt = cat0([cat1([a_pr[j], uv[j]]), cat1([jnp.zeros_like(v_c[j]), v_c[j]])])
            gh = _dot(lt, rgt)
            w_all = jnp.exp(jnp.sum(prep[ci]["lw"][:, sls[j]], axis=0, keepdims=True))
            gmat.append((jnp.where(same_head, gh[:, 0:PAIR], 0.0)
                         + jnp.where(diag2, w_all, 0.0)).astype(BF16))
            hmat.append(jnp.where(same_head, gh[:, PAIR:2 * PAIR], 0.0))
        for ci in range(group):
            ys = []
            for p in range(N_PAIRS):
                j = ci * N_PAIRS + p
                seq = _dot(cat0([r_pr[j], gmat[j]]), st_ref[p].astype(BF16))
                ys.append(seq[0:CHUNK] + y0[j])
                st_ref[p] = seq[CHUNK:CHUNK + PAIR] + hmat[j]
            y_s[prep[ci]["rows"], :] = cat1(ys)
        return carry

    lax.fori_loop(0, n_chunks // group, group_body, 0)

    y = y_s[...]
    r = rkvg_ref[0, :, 0:D_RWKV].astype(F32)
    v = rkvg_ref[0, :, 2 * D_RWKV:3 * D_RWKV].astype(F32)
    g = rkvg_ref[0, :, 3 * D_RWKV:4 * D_RWKV].astype(F32)
    inv_n = 1.0 / HEAD_DIM
    yc = y - group_sum(y) * inv_n
    yn = yc * lax.rsqrt(group_sum(yc * yc) * inv_n + GN_EPS)
    out = yn * lnx_g + lnx_b + group_sum(r * km_s[...] * r_k) * v
    o_ref[0] = (out * (g / (1.0 + jnp.exp(-g)))).astype(BF16)


def _rwkv(rkvg, wa, w2a2, vecs, ts):
    b, s, _ = rkvg.shape
    ones = jnp.asarray(
        (jnp.arange(PAIR)[:, None] // HEAD_DIM) == (jnp.arange(PAIR)[None, :] // HEAD_DIM), BF16)
    tri = jnp.asarray(jnp.arange(CHUNK)[:, None] >= jnp.arange(CHUNK)[None, :], BF16)
    return pl.pallas_call(
        _rwkv_kernel,
        grid=(b, s // ts),
        in_specs=[
            pl.BlockSpec((1, ts, 4 * D_RWKV), lambda bi, i: (bi, i, 0)),
            pl.BlockSpec((1, ts, 2 * LORA), lambda bi, i: (bi, i, 0)),
            pl.BlockSpec((2 * LORA, 2 * D_RWKV), lambda bi, i: (0, 0)),
            pl.BlockSpec((8, D_RWKV), lambda bi, i: (0, 0)),
            pl.BlockSpec((PAIR, PAIR), lambda bi, i: (0, 0)),
            pl.BlockSpec((CHUNK, CHUNK), lambda bi, i: (0, 0)),
        ],
        out_specs=pl.BlockSpec((1, ts, D_RWKV), lambda bi, i: (bi, i, 0)),
        out_shape=jax.ShapeDtypeStruct((b, s, D_RWKV), BF16),
        scratch_shapes=[
            pltpu.VMEM((N_PAIRS, PAIR, PAIR), F32),
            pltpu.VMEM((ts, D_RWKV), F32),
            pltpu.VMEM((ts, D_RWKV), F32),
            pltpu.VMEM((ts, D_RWKV), F32),
            pltpu.VMEM((ts, D_RWKV), F32),
            pltpu.VMEM((ts, D_RWKV), F32),
        ],
        compiler_params=_params(("arbitrary", "arbitrary")),
        name="rwkv7",
    )(rkvg, wa, w2a2, vecs, ones, tri)


def _attn_kernel(q_ref, k_ref, v_ref, g_ref, qg_ref, kg_ref, ones_ref, bias_ref, o_ref,
                 kp_s, vp_s, qn_s):
    i = pl.program_id(1)
    tq = q_ref.shape[1]
    s = k_ref.shape[1]
    n_chunks = tq // CHUNK
    ones = ones_ref[...]

    def head_norm(x, gain):
        hi, lo = _split2(x * x)
        parts = []
        for p in range(N_PAIRS):
            sl = slice(p * PAIR, (p + 1) * PAIR)
            parts.append(_dot(hi[:, sl], ones) + _dot(lo[:, sl], ones))
        ms = jnp.concatenate(parts, axis=1) * (1.0 / HEAD_DIM)
        return (x * lax.rsqrt(ms + RMS_EPS) * gain).astype(BF16)

    @pl.when(i == 0)
    def _():
        kp_s[0:WPAD, :] = jnp.zeros((WPAD, D_ATT), BF16)
        vp_s[0:WPAD, :] = jnp.zeros((WPAD, D_ATT), BF16)
        kp_s[WPAD:WPAD + s, :] = head_norm(k_ref[0].astype(F32), kg_ref[...])
        vp_s[WPAD:WPAD + s, :] = v_ref[0]

    lane = lax.broadcasted_iota(jnp.int32, (CHUNK, PAIR), 1)
    lane_lo = lane < HEAD_DIM
    jpos = lax.broadcasted_iota(jnp.int32, (CHUNK, WBAND), 1)
    qn_s[...] = head_norm(q_ref[0].astype(F32), qg_ref[...])

    def chunk_body(c, carry):
        r0 = pl.multiple_of(c * CHUNK, CHUNK)
        q0 = i * tq + r0
        kb = kp_s[pl.ds(pl.multiple_of(q0, CHUNK), WBAND), :]
        vb = vp_s[pl.ds(pl.multiple_of(q0, CHUNK), WBAND), :]
        valid = jpos >= jnp.maximum(WPAD - q0, WPAD - PAD)
        qc = qn_s[pl.ds(r0, CHUNK), :]
        outs = []
        for p in range(N_PAIRS):
            sl = slice(p * PAIR, (p + 1) * PAIR)
            q_p = qc[:, sl]
            sc = _dot_nt(_stack2(q_p, lane_lo), kb[:, sl]) * (HEAD_DIM ** -0.5)
            probs = []
            dens = []
            for e in range(2):
                se = sc[e * CHUNK:(e + 1) * CHUNK] + bias_ref[0, 2 * p + e]
                se = jnp.where(valid, se, -1e30)
                pe = jnp.exp(se - jnp.max(se, axis=-1, keepdims=True))
                dens.append(jnp.sum(pe, axis=-1, keepdims=True))
                probs.append(pe.astype(BF16))
            pv = _dot(jnp.concatenate(probs, axis=0), vb[:, sl])
            outs.append(jnp.where(lane_lo, pv[0:CHUNK] / dens[0], pv[CHUNK:2 * CHUNK] / dens[1]))
        o = jnp.concatenate(outs, axis=1)
        g = g_ref[0, pl.ds(r0, CHUNK), :].astype(F32)
        o_ref[0, pl.ds(r0, CHUNK), :] = (o * (g / (1.0 + jnp.exp(-g)))).astype(BF16)
        return carry

    lax.fori_loop(0, n_chunks, chunk_body, 0)


def _attn(att, q_g, k_g, bias_l, tq):
    b, s, _ = att.shape
    ones = jnp.asarray(
        (jnp.arange(PAIR)[:, None] // HEAD_DIM) == (jnp.arange(PAIR)[None, :] // HEAD_DIM), BF16)
    qg = jnp.tile(q_g, N_HEADS).reshape(1, D_ATT)
    kg = jnp.tile(k_g, N_HEADS).reshape(1, D_ATT)
    return pl.pallas_call(
        _attn_kernel,
        grid=(b, s // tq),
        in_specs=[
            pl.BlockSpec((1, tq, D_ATT), lambda bi, i: (bi, i, 0)),
            pl.BlockSpec((1, s, D_ATT), lambda bi, i: (bi, 0, 1)),
            pl.BlockSpec((1, s, D_ATT), lambda bi, i: (bi, 0, 2)),
            pl.BlockSpec((1, tq, D_ATT), lambda bi, i: (bi, i, 3)),
            pl.BlockSpec((1, D_ATT), lambda bi, i: (0, 0)),
            pl.BlockSpec((1, D_ATT), lambda bi, i: (0, 0)),
            pl.BlockSpec((PAIR, PAIR), lambda bi, i: (0, 0)),
            pl.BlockSpec((1, N_HEADS, CHUNK, WBAND), lambda bi, i: (0, 0, 0, 0)),
        ],
        out_specs=pl.BlockSpec((1, tq, D_ATT), lambda bi, i: (bi, i, 0)),
        out_shape=jax.ShapeDtypeStruct((b, s, D_ATT), BF16),
        scratch_shapes=[
            pltpu.VMEM((WPAD + s, D_ATT), BF16),
            pltpu.VMEM((WPAD + s, D_ATT), BF16),
            pltpu.VMEM((tq, D_ATT), BF16),
        ],
        compiler_params=_params(("arbitrary", "arbitrary")),
        name="band_attn",
    )(att, att, att, att, qg, kg, ones, bias_l)


def _outproj_kernel(x_ref, yr_ref, ya_ref, mod_ref, w_ref, o_ref):
    acc = _dot(yr_ref[0], w_ref[0:D_RWKV, :]) + _dot(ya_ref[0], w_ref[D_RWKV:D_RWKV + D_ATT, :])
    gate = mod_ref[0, :, 2 * D_MODEL:3 * D_MODEL]
    o_ref[0] = x_ref[0] + gate * acc


def _outproj(x, yr, ya, mod_l, w_out_l, tm):
    b, s, d = x.shape
    return pl.pallas_call(
        _outproj_kernel,
        grid=(b, s // tm),
        in_specs=[
            pl.BlockSpec((1, tm, d), lambda bi, i: (bi, i, 0)),
            pl.BlockSpec((1, tm, D_RWKV), lambda bi, i: (bi, i, 0)),
            pl.BlockSpec((1, tm, D_ATT), lambda bi, i: (bi, i, 0)),
            pl.BlockSpec((1, 1, 3 * d), lambda bi, i: (bi, 0, 0)),
            pl.BlockSpec((D_RWKV + D_ATT, d), lambda bi, i: (0, 0)),
        ],
        out_specs=pl.BlockSpec((1, tm, d), lambda bi, i: (bi, i, 0)),
        out_shape=jax.ShapeDtypeStruct((b, s, d), F32),
        compiler_params=_params(("arbitrary", "arbitrary")),
        name="outproj",
    )(x, yr, ya, mod_l.reshape(b, 1, 3 * d), w_out_l)


def _tile(s, want):
    t = min(want, s)
    assert s % t == 0 and t % CHUNK == 0
    return t


def kernel(x, c, norm_g, w_ada, b_ada, w_in, mu_shift, w0, w2, a0, a2, k_k, k_a, r_k, lnx_g, lnx_b,
           q_norm_g, k_norm_g, rel_bias, w_out):
    n_layers = w_in.shape[0]
    s = x.shape[1]
    tm = _tile(s, 512)
    ts = _tile(s, 256)
    tq = _tile(s, 256)
    mod = _ada_mod(c, w_ada, b_ada)
    bias = _bias_tables(rel_bias)
    w_in_b = w_in.astype(BF16)
    w_out_b = w_out.astype(BF16)
    zeros = jnp.zeros((LORA, D_RWKV), F32)
    for l in range(n_layers):
        rkvg, wa, att = _inproj(x, mod[l], norm_g[l], w_in_b[l], mu_shift[l], tm)
        w2a2 = jnp.concatenate([jnp.concatenate([w2[l], zeros], axis=1),
                                jnp.concatenate([zeros, a2[l]], axis=1)], axis=0).astype(BF16)
        vecs = jnp.stack([w0[l], a0[l], k_k[l], k_a[l], r_k[l].reshape(-1), lnx_g[l], lnx_b[l],
                          jnp.zeros((D_RWKV,), F32)])
        yr = _rwkv(rkvg, wa, w2a2, vecs, ts)
        ya = _attn(att, q_norm_g[l], k_norm_g[l], bias[l:l + 1], tq)
        x = _outproj(x, yr, ya, mod[l], w_out_b[l], tm)
    return x
```

```python
import functools

import jax
import jax.numpy as jnp
from jax import lax
from jax.experimental import pallas as pl
from jax.experimental.pallas import tpu as pltpu

D_MODEL = 1024
CHUNK = 64
HEAD_DIM = 64
D_RWKV = 512
D_ATT = 512
N_HEADS = 8
LORA = 64
N_LEFT = 8
PAD = N_LEFT * CHUNK
BAND = PAD + CHUNK
WPAD = PAD + CHUNK
WBAND = WPAD + CHUNK
REL_CLIP = 128
N_REL = CHUNK + REL_CLIP
RMS_EPS = 1e-6
GN_EPS = 64e-5
D_SHIFT = 4 * D_RWKV + 2 * LORA
D_IN = D_SHIFT + 4 * D_ATT
PAIR = 2 * HEAD_DIM
N_PAIRS = N_HEADS // 2
RWKV_CHUNKS_PER_STEP = 4
ATTN_CHUNKS_PER_STEP = 2
SUBLANES = 8
LOG2E = 1.4426950408889634
EXP_NEG_HALF = 0.6065306597126334
MASK_VALUE = -1e30

VMEM_LIMIT_V7X = 56 * 1024 * 1024

F32 = jnp.float32
BF16 = jnp.bfloat16


def _dot(a, b):
    return jnp.dot(a, b, preferred_element_type=F32)


def _dot_nt(a, b):
    return lax.dot_general(a, b, (((1,), (1,)), ((), ())), preferred_element_type=F32)


def _split2(x):
    hi = x.astype(BF16)
    lo = (x - hi.astype(F32)).astype(BF16)
    return hi, lo


def _split3(x):
    hi = x.astype(BF16)
    r1 = x - hi.astype(F32)
    mid = r1.astype(BF16)
    lo = (r1 - mid.astype(F32)).astype(BF16)
    return hi, mid, lo


def _params(sem):
    return pltpu.CompilerParams(dimension_semantics=sem, vmem_limit_bytes=VMEM_LIMIT_V7X)


def _ada_kernel(c_ref, w_ref, b_ref, o_ref):
    c = c_ref[...]
    ca = c / (1.0 + jnp.exp(-c))
    o_ref[0] = _dot(ca.astype(BF16), w_ref[0].astype(BF16)) + b_ref[0]


def _ada_mod(c, w_ada, b_ada):
    n_layers, d, d3 = w_ada.shape
    b = c.shape[0]
    tn = 1024
    return pl.pallas_call(
        _ada_kernel,
        grid=(n_layers, d3 // tn),
        in_specs=[
            pl.BlockSpec((b, d), lambda l, j: (0, 0)),
            pl.BlockSpec((1, d, tn), lambda l, j: (l, 0, j)),
            pl.BlockSpec((1, 1, tn), lambda l, j: (l, 0, j)),
        ],
        out_specs=pl.BlockSpec((1, b, tn), lambda l, j: (l, 0, j)),
        out_shape=jax.ShapeDtypeStruct((n_layers, b, d3), F32),
        compiler_params=_params(("arbitrary", "arbitrary")),
        name="ada_mod",
    )(c, w_ada, b_ada.reshape(n_layers, 1, d3))


def _bias_kernel(tbl_ref, o_ref):
    l = pl.program_id(0)
    h = pl.program_id(1)
    ji = lax.broadcasted_iota(jnp.int32, (SUBLANES, WBAND), 1)
    idx = jnp.clip(WPAD - ji, -(CHUNK - 1), REL_CLIP) + (CHUNK - 1)
    base = (l * N_HEADS + h) * N_REL

    def body(m, acc):
        return jnp.where(idx == m, tbl_ref[base + m], acc)

    row0 = lax.fori_loop(0, N_REL, body, jnp.zeros((SUBLANES, WBAND), F32)) * LOG2E
    for q in range(CHUNK):
        rolled = row0 if q == 0 else pltpu.roll(row0, q, 1)
        o_ref[0, 0, q:q + 1, :] = jnp.where(ji[0:1] < WPAD - PAD, MASK_VALUE, rolled[0:1])


def _bias_tables(rel_bias):
    n_layers = rel_bias.shape[0]
    return pl.pallas_call(
        _bias_kernel,
        grid=(n_layers, N_HEADS),
        in_specs=[pl.BlockSpec(memory_space=pltpu.SMEM)],
        out_specs=pl.BlockSpec((1, 1, CHUNK, WBAND), lambda l, h: (l, h, 0, 0)),
        out_shape=jax.ShapeDtypeStruct((n_layers, N_HEADS, CHUNK, WBAND), F32),
        compiler_params=_params(("arbitrary", "arbitrary")),
        name="bias_tables",
    )(rel_bias.reshape(-1))


def _inproj_kernel(x_ref, mod_ref, g_ref, w_ref, mu_ref, rkvg_ref, wa_ref, att_ref, carry_ref):
    i = pl.program_id(1)
    tm = x_ref.shape[1]

    @pl.when(i == 0)
    def _():
        carry_ref[...] = jnp.zeros_like(carry_ref)

    x = x_ref[0]
    y = x * lax.rsqrt(jnp.mean(x * x, axis=-1, keepdims=True) + RMS_EPS)
    shift = mod_ref[0, :, 0:D_MODEL]
    scale = mod_ref[0, :, D_MODEL:2 * D_MODEL]
    h = ((y * g_ref[...]) * (1.0 + scale) + shift).astype(BF16)

    p = _dot(h, w_ref[:, 0:D_SHIFT])
    row = lax.broadcasted_iota(jnp.int32, (tm, D_SHIFT), 0)
    prev = jnp.where(row == 0, carry_ref[...], pltpu.roll(p, 1, 0))
    carry_ref[...] = p[tm - 1:tm, :]
    ps = p + mu_ref[...] * (prev - p)
    rkvg_ref[0] = ps[:, 0:4 * D_RWKV].astype(BF16)
    wa_ref[0] = ps[:, 4 * D_RWKV:D_SHIFT].astype(BF16)
    att_ref[0] = _dot(h, w_ref[:, D_SHIFT:D_IN]).astype(BF16)


def _inproj(x, mod_l, norm_g_l, w_in_l, mu_l, tm):
    b, s, d = x.shape
    return pl.pallas_call(
        _inproj_kernel,
        grid=(b, s // tm),
        in_specs=[
            pl.BlockSpec((1, tm, d), lambda bi, i: (bi, i, 0)),
            pl.BlockSpec((1, 1, 3 * d), lambda bi, i: (bi, 0, 0)),
            pl.BlockSpec((1, d), lambda bi, i: (0, 0)),
            pl.BlockSpec((d, D_IN), lambda bi, i: (0, 0)),
            pl.BlockSpec((1, D_SHIFT), lambda bi, i: (0, 0)),
        ],
        out_specs=[
            pl.BlockSpec((1, tm, 4 * D_RWKV), lambda bi, i: (bi, i, 0)),
            pl.BlockSpec((1, tm, 2 * LORA), lambda bi, i: (bi, i, 0)),
            pl.BlockSpec((1, tm, 4 * D_ATT), lambda bi, i: (bi, i, 0)),
        ],
        out_shape=[
            jax.ShapeDtypeStruct((b, s, 4 * D_RWKV), BF16),
            jax.ShapeDtypeStruct((b, s, 2 * LORA), BF16),
            jax.ShapeDtypeStruct((b, s, 4 * D_ATT), BF16),
        ],
        scratch_shapes=[pltpu.VMEM((1, D_SHIFT), F32)],
        compiler_params=_params(("arbitrary", "arbitrary")),
        name="inproj",
    )(x, mod_l.reshape(b, 1, 3 * d), norm_g_l.reshape(1, d), w_in_l, mu_l.reshape(1, D_SHIFT))


def _stack2(x, lane_lo):
    zero = jnp.zeros_like(x)
    return jnp.concatenate([jnp.where(lane_lo, x, zero), jnp.where(lane_lo, zero, x)], axis=0)


def _rwkv_kernel(rkvg_ref, wa_ref, w2a2_ref, vec_ref, ones_ref, tri_ref, o_ref,
                 st_ref, lw_s, al_s, be_s, km_s, y_s):
    i = pl.program_id(1)
    ts = rkvg_ref.shape[1]
    n_chunks = ts // CHUNK

    @pl.when(i == 0)
    def _():
        st_ref[...] = jnp.zeros_like(st_ref)

    w0 = vec_ref[0:1, :]
    a0 = vec_ref[1:2, :]
    k_k = vec_ref[2:3, :]
    k_a = vec_ref[3:4, :]
    r_k = vec_ref[4:5, :]
    lnx_g = vec_ref[5:6, :]
    lnx_b = vec_ref[6:7, :]
    ones = ones_ref[...]

    def group_sum(x):
        hi, lo = _split2(x)
        parts = []
        for p in range(N_PAIRS):
            sl = slice(p * PAIR, (p + 1) * PAIR)
            parts.append(_dot(hi[:, sl], ones) + _dot(lo[:, sl], ones))
        return jnp.concatenate(parts, axis=1)

    k = rkvg_ref[0, :, D_RWKV:2 * D_RWKV].astype(F32)
    wa = wa_ref[0].astype(F32)
    lane_wa = lax.broadcasted_iota(jnp.int32, wa.shape, 1)
    tw = jnp.where(lane_wa < LORA, jnp.tanh(wa), wa).astype(BF16)
    za = _dot(tw, w2a2_ref[...])
    z = w0 + za[:, 0:D_RWKV]
    aa = a0 + za[:, D_RWKV:2 * D_RWKV]
    lw_s[...] = -EXP_NEG_HALF / (1.0 + jnp.exp(-z))
    a = 1.0 / (1.0 + jnp.exp(-aa))
    kk = k * k_k
    kk = kk * lax.rsqrt(jnp.maximum(group_sum(kk * kk), 1e-24))
    al_s[...] = -kk
    be_s[...] = kk * a
    km_s[...] = k * (1.0 + (a - 1.0) * k_a)

    row = lax.broadcasted_iota(jnp.int32, (CHUNK, PAIR), 0)
    lane = lax.broadcasted_iota(jnp.int32, (CHUNK, PAIR), 1)
    lane_lo = lane < HEAD_DIM
    col = jnp.where(lane_lo, lane, lane - HEAD_DIM)
    m_strict = col < row
    m_incl = col <= row
    eye = (col == row).astype(F32)
    row2 = lax.broadcasted_iota(jnp.int32, (PAIR, PAIR), 0)
    lane2 = lax.broadcasted_iota(jnp.int32, (PAIR, PAIR), 1)
    same_head = (row2 < HEAD_DIM) == (lane2 < HEAD_DIM)
    diag2 = row2 == lane2
    lane_lo2 = lane_lo
    tri = tri_ref[...]

    s2 = functools.partial(_stack2, lane_lo=lane_lo2)
    cat0 = functools.partial(jnp.concatenate, axis=0)
    cat1 = functools.partial(jnp.concatenate, axis=1)
    group = RWKV_CHUNKS_PER_STEP

    def group_body(gi, carry):
        prep = []
        for ci in range(group):
            r0 = pl.multiple_of((gi * group + ci) * CHUNK, CHUNK)
            rows = pl.ds(r0, CHUNK)
            r = rkvg_ref[0, rows, 0:D_RWKV].astype(F32)
            v = rkvg_ref[0, rows, 2 * D_RWKV:3 * D_RWKV]
            lw = lw_s[rows, :]
            be = be_s[rows, :]
            km = km_s[rows, :]
            hi, mid, lo = _split3(lw)
            cum = _dot(tri, hi) + _dot(tri, mid) + _dot(tri, lo)
            last = jnp.sum(lw, axis=0, keepdims=True)
            w_inv = jnp.exp(-cum)
            w_end = jnp.exp(last - cum)
            prep.append(dict(
                rows=rows, v=v, lw=lw,
                rt=r * jnp.exp(cum),
                at=(al_s[rows, :] * jnp.exp(cum - lw)).astype(BF16),
                bt=(be * w_inv).astype(BF16),
                kt=(km * w_inv).astype(BF16),
                bw=be * w_end,
                kw=km * w_end))
        chains = [(ci, p) for ci in range(group) for p in range(N_PAIRS)]
        sls = [slice(p * PAIR, (p + 1) * PAIR) for _, p in chains]
        at_c = [prep[ci]["at"][:, sl] for (ci, _), sl in zip(chains, sls)]
        rt_c = [prep[ci]["rt"][:, sl] for (ci, _), sl in zip(chains, sls)]
        v_c = [prep[ci]["v"][:, sl] for (ci, _), sl in zip(chains, sls)]
        amat = [_dot_nt(cat0([at_c[j], rt_c[j].astype(BF16)]),
                        cat0([s2(prep[ci]["bt"][:, sls[j]]), s2(prep[ci]["kt"][:, sls[j]])]))
                for j, (ci, _) in enumerate(chains)]
        n1 = [jnp.where(m_strict, a[0:CHUNK, 0:PAIR], 0.0) for a in amat]
        a_ak = [jnp.where(m_strict, a[0:CHUNK, PAIR:2 * PAIR], 0.0).astype(BF16) for a in amat]
        a_rb = [jnp.where(m_incl, a[CHUNK:2 * CHUNK, 0:PAIR], 0.0).astype(BF16) for a in amat]
        a_rk = [jnp.where(m_incl, a[CHUNK:2 * CHUNK, PAIR:2 * PAIR], 0.0).astype(BF16) for a in amat]
        n1b = [n.astype(BF16) for n in n1]
        npow = [_dot(n, s2(n)) for n in n1b]
        av = [_dot(cat0([a_ak[j], a_rk[j]]), s2(v_c[j])) for j in range(len(chains))]
        tmat = [eye + n for n in n1]
        for _ in range(4):
            npb = [n.astype(BF16) for n in npow]
            both = [_dot(cat0([t.astype(BF16), n]), s2(n)) for t, n in zip(tmat, npb)]
            tmat = [t + b[0:CHUNK] for t, b in zip(tmat, both)]
            npow = [b[CHUNK:2 * CHUNK] for b in both]
        tmat = [t + _dot(t.astype(BF16), s2(n.astype(BF16))) for t, n in zip(tmat, npow)]
        t2 = [_dot(tmat[j].astype(BF16), cat1([s2(at_c[j]), s2(av[j][0:CHUNK].astype(BF16))]))
              for j in range(len(chains))]
        a_pr = [t[:, 0:PAIR].astype(BF16) for t in t2]
        uv = [t[:, PAIR:2 * PAIR].astype(BF16) for t in t2]
        r2 = [_dot(a_rb[j], cat1([s2(a_pr[j]), s2(uv[j])])) for j in range(len(chains))]
        r_pr = [(rt_c[j] + r2[j][:, 0:PAIR]).astype(BF16) for j in range(len(chains))]
        y0 = [av[j][CHUNK:2 * CHUNK] + r2[j][:, PAIR:2 * PAIR] for j in range(len(chains))]
        gmat, hmat = [], []
        for j, (ci, _) in enumerate(chains):
            lt = cat0([prep[ci]["bw"][:, sls[j]], prep[ci]["kw"][:, sls[j]]]).T.astype(BF16)
            rgt = cat0([cat1([a_pr[j], uv[j]]), cat1([jnp.zeros_like(v_c[j]), v_c[j]])])
            gh = _dot(lt, rgt)
            w_all = jnp.exp(jnp.sum(prep[ci]["lw"][:, sls[j]], axis=0, keepdims=True))
            gmat.append((jnp.where(same_head, gh[:, 0:PAIR], 0.0)
                         + jnp.where(diag2, w_all, 0.0)).astype(BF16))
            hmat.append(jnp.where(same_head, gh[:, PAIR:2 * PAIR], 0.0))
        for ci in range(group):
            ys = []
            for p in range(N_PAIRS):
                j = ci * N_PAIRS + p
                seq = _dot(cat0([r_pr[j], gmat[j]]), st_ref[p].astype(BF16))
                ys.append(seq[0:CHUNK] + y0[j])
                st_ref[p] = seq[CHUNK:CHUNK + PAIR] + hmat[j]
            y_s[prep[ci]["rows"], :] = cat1(ys)
        return carry

    lax.fori_loop(0, n_chunks // group, group_body, 0)

    y = y_s[...]
    r = rkvg_ref[0, :, 0:D_RWKV].astype(F32)
    v = rkvg_ref[0, :, 2 * D_RWKV:3 * D_RWKV].astype(F32)
    g = rkvg_ref[0, :, 3 * D_RWKV:4 * D_RWKV].astype(F32)
    inv_n = 1.0 / HEAD_DIM
    yc = y - group_sum(y) * inv_n
    yn = yc * lax.rsqrt(group_sum(yc * yc) * inv_n + GN_EPS)
    out = yn * lnx_g + lnx_b + group_sum(r * km_s[...] * r_k) * v
    o_ref[0] = (out * (g / (1.0 + jnp.exp(-g)))).astype(BF16)


def _rwkv(rkvg, wa, w2a2, vecs, ts):
    b, s, _ = rkvg.shape
    ones = jnp.asarray(
        (jnp.arange(PAIR)[:, None] // HEAD_DIM) == (jnp.arange(PAIR)[None, :] // HEAD_DIM), BF16)
    tri = jnp.asarray(jnp.arange(CHUNK)[:, None] >= jnp.arange(CHUNK)[None, :], BF16)
    return pl.pallas_call(
        _rwkv_kernel,
        grid=(b, s // ts),
        in_specs=[
            pl.BlockSpec((1, ts, 4 * D_RWKV), lambda bi, i: (bi, i, 0)),
            pl.BlockSpec((1, ts, 2 * LORA), lambda bi, i: (bi, i, 0)),
            pl.BlockSpec((2 * LORA, 2 * D_RWKV), lambda bi, i: (0, 0)),
            pl.BlockSpec((8, D_RWKV), lambda bi, i: (0, 0)),
            pl.BlockSpec((PAIR, PAIR), lambda bi, i: (0, 0)),
            pl.BlockSpec((CHUNK, CHUNK), lambda bi, i: (0, 0)),
        ],
        out_specs=pl.BlockSpec((1, ts, D_RWKV), lambda bi, i: (bi, i, 0)),
        out_shape=jax.ShapeDtypeStruct((b, s, D_RWKV), BF16),
        scratch_shapes=[
            pltpu.VMEM((N_PAIRS, PAIR, PAIR), F32),
            pltpu.VMEM((ts, D_RWKV), F32),
            pltpu.VMEM((ts, D_RWKV), F32),
            pltpu.VMEM((ts, D_RWKV), F32),
            pltpu.VMEM((ts, D_RWKV), F32),
            pltpu.VMEM((ts, D_RWKV), F32),
        ],
        compiler_params=_params(("arbitrary", "arbitrary")),
        name="rwkv7",
    )(rkvg, wa, w2a2, vecs, ones, tri)


def _attn_kernel(q_ref, k_ref, v_ref, g_ref, qg_ref, kg_ref, ones_ref, bias_ref, o_ref,
                 kp_s, vp_s, qn_s):
    i = pl.program_id(1)
    tq = q_ref.shape[1]
    s = k_ref.shape[1]
    n_chunks = tq // CHUNK
    ones = ones_ref[...]

    def head_norm(x, gain):
        hi, lo = _split2(x * x)
        parts = []
        for p in range(N_PAIRS):
            sl = slice(p * PAIR, (p + 1) * PAIR)
            parts.append(_dot(hi[:, sl], ones) + _dot(lo[:, sl], ones))
        ms = jnp.concatenate(parts, axis=1) * (1.0 / HEAD_DIM)
        return (x * lax.rsqrt(ms + RMS_EPS) * gain).astype(BF16)

    @pl.when(i == 0)
    def _():
        kp_s[0:WPAD, :] = jnp.zeros((WPAD, D_ATT), BF16)
        vp_s[0:WPAD, :] = jnp.zeros((WPAD, D_ATT), BF16)
        kp_s[WPAD:WPAD + s, :] = head_norm(k_ref[0].astype(F32), kg_ref[...])
        vp_s[WPAD:WPAD + s, :] = v_ref[0]

    lane = lax.broadcasted_iota(jnp.int32, (CHUNK, PAIR), 1)
    lane_lo = lane < HEAD_DIM
    jpos = lax.broadcasted_iota(jnp.int32, (CHUNK, WBAND), 1)
    qn_s[...] = head_norm(q_ref[0].astype(F32), qg_ref[...] * (HEAD_DIM ** -0.5 * LOG2E))
    group = ATTN_CHUNKS_PER_STEP
    pair_sl = [slice(p * PAIR, (p + 1) * PAIR) for p in range(N_PAIRS)]

    def group_body(gi, carry):
        r0s = [pl.multiple_of((gi * group + ci) * CHUNK, CHUNK) for ci in range(group)]
        q0s = [i * tq + r0 for r0 in r0s]
        wins = [pl.ds(pl.multiple_of(q0, CHUNK), WBAND) for q0 in q0s]
        sc = []
        for ci in range(group):
            qc = qn_s[pl.ds(r0s[ci], CHUNK), :]
            for sl in pair_sl:
                sc.append(_dot_nt(_stack2(qc[:, sl], lane_lo), kp_s[wins[ci], sl]))
        probs, dens = [], []
        for ci in range(group):
            valid = jpos >= WPAD - q0s[ci]
            for p in range(N_PAIRS):
                pes = []
                for e in range(2):
                    se = sc[ci * N_PAIRS + p][e * CHUNK:(e + 1) * CHUNK] + bias_ref[0, 2 * p + e]
                    se = jnp.where(valid, se, MASK_VALUE)
                    pe = jnp.exp2(se - jnp.max(se, axis=-1, keepdims=True))
                    dens.append(jnp.sum(pe, axis=-1, keepdims=True))
                    pes.append(pe.astype(BF16))
                probs.append(jnp.concatenate(pes, axis=0))
        pv = [_dot(probs[ci * N_PAIRS + p], vp_s[wins[ci], pair_sl[p]])
              for ci in range(group) for p in range(N_PAIRS)]
        for ci in range(group):
            outs = []
            for p in range(N_PAIRS):
                j = ci * N_PAIRS + p
                outs.append(jnp.where(lane_lo, pv[j][0:CHUNK] / dens[2 * j],
                                      pv[j][CHUNK:2 * CHUNK] / dens[2 * j + 1]))
            o = jnp.concatenate(outs, axis=1)
            g = g_ref[0, pl.ds(r0s[ci], CHUNK), :].astype(F32)
            o_ref[0, pl.ds(r0s[ci], CHUNK), :] = (o * (g / (1.0 + jnp.exp(-g)))).astype(BF16)
        return carry

    lax.fori_loop(0, n_chunks // group, group_body, 0)


def _attn(att, q_g, k_g, bias_l, tq):
    b, s, _ = att.shape
    ones = jnp.asarray(
        (jnp.arange(PAIR)[:, None] // HEAD_DIM) == (jnp.arange(PAIR)[None, :] // HEAD_DIM), BF16)
    qg = jnp.tile(q_g, N_HEADS).reshape(1, D_ATT)
    kg = jnp.tile(k_g, N_HEADS).reshape(1, D_ATT)
    return pl.pallas_call(
        _attn_kernel,
        grid=(b, s // tq),
        in_specs=[
            pl.BlockSpec((1, tq, D_ATT), lambda bi, i: (bi, i, 0)),
            pl.BlockSpec((1, s, D_ATT), lambda bi, i: (bi, 0, 1)),
            pl.BlockSpec((1, s, D_ATT), lambda bi, i: (bi, 0, 2)),
            pl.BlockSpec((1, tq, D_ATT), lambda bi, i: (bi, i, 3)),
            pl.BlockSpec((1, D_ATT), lambda bi, i: (0, 0)),
            pl.BlockSpec((1, D_ATT), lambda bi, i: (0, 0)),
            pl.BlockSpec((PAIR, PAIR), lambda bi, i: (0, 0)),
            pl.BlockSpec((1, N_HEADS, CHUNK, WBAND), lambda bi, i: (0, 0, 0, 0)),
        ],
        out_specs=pl.BlockSpec((1, tq, D_ATT), lambda bi, i: (bi, i, 0)),
        out_shape=jax.ShapeDtypeStruct((b, s, D_ATT), BF16),
        scratch_shapes=[
            pltpu.VMEM((WPAD + s, D_ATT), BF16),
            pltpu.VMEM((WPAD + s, D_ATT), BF16),
            pltpu.VMEM((tq, D_ATT), BF16),
        ],
        compiler_params=_params(("arbitrary", "arbitrary")),
        name="band_attn",
    )(att, att, att, att, qg, kg, ones, bias_l)


def _outproj_kernel(x_ref, yr_ref, ya_ref, mod_ref, w_ref, o_ref):
    acc = _dot(yr_ref[0], w_ref[0:D_RWKV, :]) + _dot(ya_ref[0], w_ref[D_RWKV:D_RWKV + D_ATT, :])
    gate = mod_ref[0, :, 2 * D_MODEL:3 * D_MODEL]
    o_ref[0] = x_ref[0] + gate * acc


def _outproj(x, yr, ya, mod_l, w_out_l, tm):
    b, s, d = x.shape
    return pl.pallas_call(
        _outproj_kernel,
        grid=(b, s // tm),
        in_specs=[
            pl.BlockSpec((1, tm, d), lambda bi, i: (bi, i, 0)),
            pl.BlockSpec((1, tm, D_RWKV), lambda bi, i: (bi, i, 0)),
            pl.BlockSpec((1, tm, D_ATT), lambda bi, i: (bi, i, 0)),
            pl.BlockSpec((1, 1, 3 * d), lambda bi, i: (bi, 0, 0)),
            pl.BlockSpec((D_RWKV + D_ATT, d), lambda bi, i: (0, 0)),
        ],
        out_specs=pl.BlockSpec((1, tm, d), lambda bi, i: (bi, i, 0)),
        out_shape=jax.ShapeDtypeStruct((b, s, d), F32),
        compiler_params=_params(("arbitrary", "arbitrary")),
        name="outproj",
    )(x, yr, ya, mod_l.reshape(b, 1, 3 * d), w_out_l)


def _tile(s, want):
    t = min(want, s)
    assert s % t == 0 and t % CHUNK == 0
    return t


def kernel(x, c, norm_g, w_ada, b_ada, w_in, mu_shift, w0, w2, a0, a2, k_k, k_a, r_k, lnx_g, lnx_b,
           q_norm_g, k_norm_g, rel_bias, w_out):
    n_layers = w_in.shape[0]
    s = x.shape[1]
    tm = _tile(s, 512)
    ts = _tile(s, 256)
    tq = _tile(s, 256)
    mod = _ada_mod(c, w_ada, b_ada)
    bias = _bias_tables(rel_bias)
    w_in_b = w_in.astype(BF16)
    w_out_b = w_out.astype(BF16)
    zeros = jnp.zeros((LORA, D_RWKV), F32)
    for l in range(n_layers):
        rkvg, wa, att = _inproj(x, mod[l], norm_g[l], w_in_b[l], mu_shift[l], tm)
        w2a2 = jnp.concatenate([jnp.concatenate([w2[l], zeros], axis=1),
                                jnp.concatenate([zeros, a2[l]], axis=1)], axis=0).astype(BF16)
        vecs = jnp.stack([w0[l], a0[l], k_k[l], k_a[l], r_k[l].reshape(-1), lnx_g[l], lnx_b[l],
                          jnp.zeros((D_RWKV,), F32)])
        yr = _rwkv(rkvg, wa, w2a2, vecs, ts)
        ya = _attn(att, q_norm_g[l], k_norm_g[l], bias[l:l + 1], tq)
        x = _outproj(x, yr, ya, mod[l], w_out_b[l], tm)
    return x
```

```python
import functools

import jax
import jax.numpy as jnp
from jax import lax
from jax.experimental import pallas as pl
from jax.experimental.pallas import tpu as pltpu

D_MODEL = 1024
CHUNK = 64
HEAD_DIM = 64
D_RWKV = 512
D_ATT = 512
N_HEADS = 8
LORA = 64
N_LEFT = 8
PAD = N_LEFT * CHUNK
BAND = PAD + CHUNK
WPAD = PAD + CHUNK
WBAND = WPAD + CHUNK
REL_CLIP = 128
N_REL = CHUNK + REL_CLIP
RMS_EPS = 1e-6
GN_EPS = 64e-5
D_SHIFT = 4 * D_RWKV + 2 * LORA
D_IN = D_SHIFT + 4 * D_ATT
PAIR = 2 * HEAD_DIM
N_PAIRS = N_HEADS // 2
QUAD = PAIR
N_QUADS = D_RWKV // QUAD
RWKV_CHUNKS_PER_STEP = 4
ATTN_CHUNKS_PER_STEP = 2
SUBLANES = 8
LOG2E = 1.4426950408889634
EXP_NEG_HALF = 0.6065306597126334
MASK_VALUE = -1e30

VMEM_LIMIT_V7X = 56 * 1024 * 1024

F32 = jnp.float32
BF16 = jnp.bfloat16


def _dot(a, b):
    return jnp.dot(a, b, preferred_element_type=F32)


def _dot_nt(a, b):
    return lax.dot_general(a, b, (((1,), (1,)), ((), ())), preferred_element_type=F32)


def _sigmoid(x):
    return 0.5 * jnp.tanh(0.5 * x) + 0.5


def _split2(x):
    hi = x.astype(BF16)
    lo = (x - hi.astype(F32)).astype(BF16)
    return hi, lo


def _split3(x):
    hi = x.astype(BF16)
    r1 = x - hi.astype(F32)
    mid = r1.astype(BF16)
    lo = (r1 - mid.astype(F32)).astype(BF16)
    return hi, mid, lo


def _params(sem):
    return pltpu.CompilerParams(dimension_semantics=sem, vmem_limit_bytes=VMEM_LIMIT_V7X)


def _ada_kernel(c_ref, w_ref, b_ref, o_ref):
    c = c_ref[...]
    ca = c / (1.0 + jnp.exp(-c))
    o_ref[0] = _dot(ca.astype(BF16), w_ref[0].astype(BF16)) + b_ref[0]


def _ada_mod(c, w_ada, b_ada):
    n_layers, d, d3 = w_ada.shape
    b = c.shape[0]
    tn = 1024
    return pl.pallas_call(
        _ada_kernel,
        grid=(n_layers, d3 // tn),
        in_specs=[
            pl.BlockSpec((b, d), lambda l, j: (0, 0)),
            pl.BlockSpec((1, d, tn), lambda l, j: (l, 0, j)),
            pl.BlockSpec((1, 1, tn), lambda l, j: (l, 0, j)),
        ],
        out_specs=pl.BlockSpec((1, b, tn), lambda l, j: (l, 0, j)),
        out_shape=jax.ShapeDtypeStruct((n_layers, b, d3), F32),
        compiler_params=_params(("arbitrary", "arbitrary")),
        name="ada_mod",
    )(c, w_ada, b_ada.reshape(n_layers, 1, d3))


def _bias_kernel(tbl_ref, o_ref):
    l = pl.program_id(0)
    h = pl.program_id(1)
    ji = lax.broadcasted_iota(jnp.int32, (SUBLANES, WBAND), 1)
    idx = jnp.clip(WPAD - ji, -(CHUNK - 1), REL_CLIP) + (CHUNK - 1)
    base = (l * N_HEADS + h) * N_REL

    def body(m, acc):
        return jnp.where(idx == m, tbl_ref[base + m], acc)

    row0 = lax.fori_loop(0, N_REL, body, jnp.zeros((SUBLANES, WBAND), F32)) * LOG2E
    for q in range(CHUNK):
        rolled = row0 if q == 0 else pltpu.roll(row0, q, 1)
        o_ref[0, 0, q:q + 1, :] = jnp.where(ji[0:1] < WPAD - PAD, MASK_VALUE, rolled[0:1])


def _bias_tables(rel_bias):
    n_layers = rel_bias.shape[0]
    return pl.pallas_call(
        _bias_kernel,
        grid=(n_layers, N_HEADS),
        in_specs=[pl.BlockSpec(memory_space=pltpu.SMEM)],
        out_specs=pl.BlockSpec((1, 1, CHUNK, WBAND), lambda l, h: (l, h, 0, 0)),
        out_shape=jax.ShapeDtypeStruct((n_layers, N_HEADS, CHUNK, WBAND), F32),
        compiler_params=_params(("arbitrary", "arbitrary")),
        name="bias_tables",
    )(rel_bias.reshape(-1))


def _inproj_kernel(x_ref, mod_ref, g_ref, w_ref, mu_ref, rkvg_ref, wa_ref, att_ref, carry_ref):
    i = pl.program_id(1)
    tm = x_ref.shape[1]

    @pl.when(i == 0)
    def _():
        carry_ref[...] = jnp.zeros_like(carry_ref)

    x = x_ref[0]
    y = x * lax.rsqrt(jnp.mean(x * x, axis=-1, keepdims=True) + RMS_EPS)
    shift = mod_ref[0, :, 0:D_MODEL]
    scale = mod_ref[0, :, D_MODEL:2 * D_MODEL]
    h = ((y * g_ref[...]) * (1.0 + scale) + shift).astype(BF16)

    p = _dot(h, w_ref[:, 0:D_SHIFT])
    row = lax.broadcasted_iota(jnp.int32, (tm, D_SHIFT), 0)
    prev = jnp.where(row == 0, carry_ref[...], pltpu.roll(p, 1, 0))
    carry_ref[...] = p[tm - 1:tm, :]
    ps = p + mu_ref[...] * (prev - p)
    rkvg_ref[0] = ps[:, 0:4 * D_RWKV].astype(BF16)
    wa_ref[0] = ps[:, 4 * D_RWKV:D_SHIFT].astype(BF16)
    att_ref[0] = _dot(h, w_ref[:, D_SHIFT:D_IN]).astype(BF16)


def _inproj(x, mod_l, norm_g_l, w_in_all, layer, mu_l, tm):
    b, s, d = x.shape
    return pl.pallas_call(
        _inproj_kernel,
        grid=(b, s // tm),
        in_specs=[
            pl.BlockSpec((1, tm, d), lambda bi, i: (bi, i, 0)),
            pl.BlockSpec((1, 1, 3 * d), lambda bi, i: (bi, 0, 0)),
            pl.BlockSpec((1, d), lambda bi, i: (0, 0)),
            pl.BlockSpec((None, d, D_IN), lambda bi, i: (layer, 0, 0)),
            pl.BlockSpec((1, D_SHIFT), lambda bi, i: (0, 0)),
        ],
        out_specs=[
            pl.BlockSpec((1, tm, 4 * D_RWKV), lambda bi, i: (bi, i, 0)),
            pl.BlockSpec((1, tm, 2 * LORA), lambda bi, i: (bi, i, 0)),
            pl.BlockSpec((1, tm, 4 * D_ATT), lambda bi, i: (bi, i, 0)),
        ],
        out_shape=[
            jax.ShapeDtypeStruct((b, s, 4 * D_RWKV), BF16),
            jax.ShapeDtypeStruct((b, s, 2 * LORA), BF16),
            jax.ShapeDtypeStruct((b, s, 4 * D_ATT), BF16),
        ],
        scratch_shapes=[pltpu.VMEM((1, D_SHIFT), F32)],
        compiler_params=_params(("arbitrary", "arbitrary")),
        name="inproj",
    )(x, mod_l.reshape(b, 1, 3 * d), norm_g_l.reshape(1, d), w_in_all, mu_l.reshape(1, D_SHIFT))


def _stack2(x, lane_lo):
    zero = jnp.zeros_like(x)
    return jnp.concatenate([jnp.where(lane_lo, x, zero), jnp.where(lane_lo, zero, x)], axis=0)


def _rwkv_kernel(rkvg_ref, wa_ref, w2a2_ref, vec_ref, ones_ref, tri_ref, o_ref,
                 st_ref, lw_s, al_s, be_s, km_s, y_s):
    i = pl.program_id(1)
    ts = rkvg_ref.shape[1]
    n_chunks = ts // CHUNK

    @pl.when(i == 0)
    def _():
        st_ref[...] = jnp.zeros_like(st_ref)

    w0 = vec_ref[0:1, :]
    a0 = vec_ref[1:2, :]
    k_k = vec_ref[2:3, :]
    k_a = vec_ref[3:4, :]
    r_k = vec_ref[4:5, :]
    lnx_g = vec_ref[5:6, :]
    lnx_b = vec_ref[6:7, :]
    ones = ones_ref[...]

    def group_sum(x, exact=False):
        hi, lo = _split2(x)
        parts = []
        for p in range(N_PAIRS):
            sl = slice(p * PAIR, (p + 1) * PAIR)
            acc = _dot(hi[:, sl], ones)
            parts.append(acc + _dot(lo[:, sl], ones) if exact else acc)
        return jnp.concatenate(parts, axis=1)

    def prologue(rows):
        k = rkvg_ref[0, rows, D_RWKV:2 * D_RWKV].astype(F32)
        wa = wa_ref[0, rows, :].astype(F32)
        lane_wa = lax.broadcasted_iota(jnp.int32, wa.shape, 1)
        tw = jnp.where(lane_wa < LORA, jnp.tanh(wa), wa).astype(BF16)
        za = _dot(tw, w2a2_ref[...])
        z = w0 + za[:, 0:D_RWKV]
        aa = a0 + za[:, D_RWKV:2 * D_RWKV]
        lw_s[rows, :] = -EXP_NEG_HALF * _sigmoid(z)
        a = _sigmoid(aa)
        kk = k * k_k
        kk = kk * lax.rsqrt(jnp.maximum(group_sum(kk * kk, exact=True), 1e-24))
        al_s[rows, :] = -kk
        be_s[rows, :] = kk * a
        km_s[rows, :] = k * (1.0 + (a - 1.0) * k_a)

    row2 = lax.broadcasted_iota(jnp.int32, (PAIR, PAIR), 0)
    lane2 = lax.broadcasted_iota(jnp.int32, (PAIR, PAIR), 1)
    same_head = (row2 < HEAD_DIM) == (lane2 < HEAD_DIM)
    diag2 = row2 == lane2
    tri = tri_ref[...]

    cat0 = functools.partial(jnp.concatenate, axis=0)
    cat1 = functools.partial(jnp.concatenate, axis=1)
    group = RWKV_CHUNKS_PER_STEP
    rowq = lax.broadcasted_iota(jnp.int32, (CHUNK, QUAD), 0)
    laneq = lax.broadcasted_iota(jnp.int32, (CHUNK, QUAD), 1)
    colq = laneq & (HEAD_DIM - 1)
    mq_strict = colq < rowq
    mq_incl = colq <= rowq
    eyeq = (colq == rowq).astype(F32)
    head_masks = [(laneq >= e * HEAD_DIM) & (laneq < (e + 1) * HEAD_DIM) for e in range(QUAD // HEAD_DIM)]

    def s4(x):
        zero = jnp.zeros_like(x)
        return cat0([jnp.where(m, x, zero) for m in head_masks])

    def chains_part(gi):
        prep = []
        for ci in range(group):
            r0 = (gi * group + ci) * CHUNK
            rows = slice(r0, r0 + CHUNK)
            r = rkvg_ref[0, rows, 0:D_RWKV].astype(F32)
            v = rkvg_ref[0, rows, 2 * D_RWKV:3 * D_RWKV]
            lw = lw_s[rows, :]
            hi, lo = _split2(lw)
            cum = _dot(tri, hi) + _dot(tri, lo)
            w_inv = jnp.exp(-cum)
            w_all = jnp.exp(jnp.sum(lw, axis=0, keepdims=True))
            bt = be_s[rows, :] * w_inv
            kt = km_s[rows, :] * w_inv
            prep.append(dict(
                rows=rows, v=v,
                rt=r * jnp.exp(cum),
                at=(al_s[rows, :] * jnp.exp(cum - lw)).astype(BF16),
                bt=bt.astype(BF16),
                kt=kt.astype(BF16),
                bw=bt * w_all,
                kw=kt * w_all,
                w_all=w_all))
        chains = [(ci, q) for ci in range(group) for q in range(N_QUADS)]
        n_ch = len(chains)
        sls = [slice(q * QUAD, (q + 1) * QUAD) for _, q in chains]
        at_c = [prep[ci]["at"][:, sl] for (ci, _), sl in zip(chains, sls)]
        rt_c = [prep[ci]["rt"][:, sl] for (ci, _), sl in zip(chains, sls)]
        v_c = [prep[ci]["v"][:, sl] for (ci, _), sl in zip(chains, sls)]
        amat = [_dot_nt(cat0([at_c[j], rt_c[j].astype(BF16)]),
                        cat0([s4(prep[ci]["bt"][:, sls[j]]), s4(prep[ci]["kt"][:, sls[j]])]))
                for j, (ci, _) in enumerate(chains)]
        n1 = [jnp.where(mq_strict, a[0:CHUNK, 0:QUAD], 0.0) for a in amat]
        a_ak = [jnp.where(mq_strict, a[0:CHUNK, QUAD:2 * QUAD], 0.0).astype(BF16) for a in amat]
        a_rb = [jnp.where(mq_incl, a[CHUNK:2 * CHUNK, 0:QUAD], 0.0).astype(BF16) for a in amat]
        a_rk = [jnp.where(mq_incl, a[CHUNK:2 * CHUNK, QUAD:2 * QUAD], 0.0).astype(BF16) for a in amat]
        n1b = [n.astype(BF16) for n in n1]
        npow = [_dot(n, s4(n)) for n in n1b]
        av = [_dot(cat0([a_ak[j], a_rk[j]]), s4(v_c[j])) for j in range(n_ch)]
        tmat = [eyeq + n for n in n1]
        for _ in range(4):
            npb = [n.astype(BF16) for n in npow]
            both = [_dot(cat0([t.astype(BF16), n]), s4(n)) for t, n in zip(tmat, npb)]
            tmat = [t + b[0:CHUNK] for t, b in zip(tmat, both)]
            npow = [b[CHUNK:2 * CHUNK] for b in both]
        tmat = [t + _dot(t.astype(BF16), s4(n.astype(BF16))) for t, n in zip(tmat, npow)]
        t2 = [_dot(tmat[j].astype(BF16), cat1([s4(at_c[j]), s4(av[j][0:CHUNK].astype(BF16))]))
              for j in range(n_ch)]
        a_pr = [t[:, 0:QUAD].astype(BF16) for t in t2]
        uv = [t[:, QUAD:2 * QUAD].astype(BF16) for t in t2]
        r2 = [_dot(a_rb[j], cat1([s4(a_pr[j]), s4(uv[j])])) for j in range(n_ch)]
        r_pr = [(rt_c[j] + r2[j][:, 0:QUAD]).astype(BF16) for j in range(n_ch)]
        y0 = [av[j][CHUNK:2 * CHUNK] + r2[j][:, QUAD:2 * QUAD] for j in range(n_ch)]
        gmat, hmat = {}, {}
        for j, (ci, q) in enumerate(chains):
            for h in range(QUAD // PAIR):
                p = q * (QUAD // PAIR) + h
                psl = slice(p * PAIR, (p + 1) * PAIR)
                hsl = slice(h * PAIR, (h + 1) * PAIR)
                v_p = prep[ci]["v"][:, psl]
                lt = cat0([prep[ci]["bw"][:, psl], prep[ci]["kw"][:, psl]]).T.astype(BF16)
                rgt = cat0([cat1([a_pr[j][:, hsl], uv[j][:, hsl]]),
                            cat1([jnp.zeros_like(v_p), v_p])])
                gh = _dot(lt, rgt)
                gmat[ci, p] = (jnp.where(same_head, gh[:, 0:PAIR], 0.0)
                               + jnp.where(diag2, prep[ci]["w_all"][:, psl], 0.0)).astype(BF16)
                hmat[ci, p] = jnp.where(same_head, gh[:, PAIR:2 * PAIR], 0.0)
        for ci in range(group):
            ys = []
            for p in range(N_PAIRS):
                j = ci * N_QUADS + p // (QUAD // PAIR)
                hsl = slice((p % (QUAD // PAIR)) * PAIR, (p % (QUAD // PAIR) + 1) * PAIR)
                seq = _dot(cat0([r_pr[j][:, hsl], gmat[ci, p]]), st_ref[p].astype(BF16))
                ys.append(seq[0:CHUNK] + y0[j][:, hsl])
                st_ref[p] = seq[CHUNK:CHUNK + PAIR] + hmat[ci, p]
            y_s[prep[ci]["rows"], :] = cat1(ys)

    def epilogue(rows):
        y = y_s[rows, :]
        r = rkvg_ref[0, rows, 0:D_RWKV].astype(F32)
        v = rkvg_ref[0, rows, 2 * D_RWKV:3 * D_RWKV].astype(F32)
        g = rkvg_ref[0, rows, 3 * D_RWKV:4 * D_RWKV].astype(F32)
        inv_n = 1.0 / HEAD_DIM
        yc = y - group_sum(y, exact=True) * inv_n
        yn = yc * lax.rsqrt(group_sum(yc * yc, exact=True) * inv_n + GN_EPS)
        out = yn * lnx_g + lnx_b + group_sum(r * km_s[rows, :] * r_k, exact=True) * v
        o_ref[0, rows, :] = (out * (g * _sigmoid(g))).astype(BF16)

    part_rows = group * CHUNK
    n_parts = ts // part_rows
    parts = [slice(k * part_rows, (k + 1) * part_rows) for k in range(n_parts)]
    for k in range(n_parts + 2):
        if k < n_parts:
            prologue(parts[k])
        if 0 <= k - 1 < n_parts:
            chains_part(k - 1)
        if 0 <= k - 2 < n_parts:
            epilogue(parts[k - 2])


def _rwkv(rkvg, wa, w2a2, vecs, ts):
    b, s, _ = rkvg.shape
    ones = jnp.asarray(
        (jnp.arange(PAIR)[:, None] // HEAD_DIM) == (jnp.arange(PAIR)[None, :] // HEAD_DIM), BF16)
    tri = jnp.asarray(jnp.arange(CHUNK)[:, None] >= jnp.arange(CHUNK)[None, :], BF16)
    return pl.pallas_call(
        _rwkv_kernel,
        grid=(b, s // ts),
        in_specs=[
            pl.BlockSpec((1, ts, 4 * D_RWKV), lambda bi, i: (bi, i, 0)),
            pl.BlockSpec((1, ts, 2 * LORA), lambda bi, i: (bi, i, 0)),
            pl.BlockSpec((2 * LORA, 2 * D_RWKV), lambda bi, i: (0, 0)),
            pl.BlockSpec((8, D_RWKV), lambda bi, i: (0, 0)),
            pl.BlockSpec((PAIR, PAIR), lambda bi, i: (0, 0)),
            pl.BlockSpec((CHUNK, CHUNK), lambda bi, i: (0, 0)),
        ],
        out_specs=pl.BlockSpec((1, ts, D_RWKV), lambda bi, i: (bi, i, 0)),
        out_shape=jax.ShapeDtypeStruct((b, s, D_RWKV), BF16),
        scratch_shapes=[
            pltpu.VMEM((N_PAIRS, PAIR, PAIR), F32),
            pltpu.VMEM((ts, D_RWKV), F32),
            pltpu.VMEM((ts, D_RWKV), F32),
            pltpu.VMEM((ts, D_RWKV), F32),
            pltpu.VMEM((ts, D_RWKV), F32),
            pltpu.VMEM((ts, D_RWKV), F32),
        ],
        compiler_params=_params(("arbitrary", "arbitrary")),
        name="rwkv7",
    )(rkvg, wa, w2a2, vecs, ones, tri)


def _attn_kernel(q_ref, k_ref, v_ref, g_ref, qg_ref, kg_ref, ones_ref, bias_ref, o_ref,
                 kp_s, vp_s, qn_s):
    i = pl.program_id(1)
    tq = q_ref.shape[1]
    s = k_ref.shape[1]
    n_chunks = tq // CHUNK
    ones = ones_ref[...]

    def head_norm(x, gain):
        hi, lo = _split2(x * x)
        parts = []
        for p in range(N_PAIRS):
            sl = slice(p * PAIR, (p + 1) * PAIR)
            parts.append(_dot(hi[:, sl], ones) + _dot(lo[:, sl], ones))
        ms = jnp.concatenate(parts, axis=1) * (1.0 / HEAD_DIM)
        return (x * lax.rsqrt(ms + RMS_EPS) * gain).astype(BF16)

    @pl.when(i == 0)
    def _():
        kp_s[0:WPAD, :] = jnp.zeros((WPAD, D_ATT), BF16)
        vp_s[0:WPAD, :] = jnp.zeros((WPAD, D_ATT), BF16)
        kp_s[WPAD:WPAD + s, :] = head_norm(k_ref[0].astype(F32), kg_ref[...])
        vp_s[WPAD:WPAD + s, :] = v_ref[0]

    lane = lax.broadcasted_iota(jnp.int32, (CHUNK, PAIR), 1)
    lane_lo = lane < HEAD_DIM
    jpos = lax.broadcasted_iota(jnp.int32, (CHUNK, WBAND), 1)
    qn_s[...] = head_norm(q_ref[0].astype(F32), qg_ref[...] * (HEAD_DIM ** -0.5 * LOG2E))
    group = ATTN_CHUNKS_PER_STEP
    pair_sl = [slice(p * PAIR, (p + 1) * PAIR) for p in range(N_PAIRS)]

    def scores(ci):
        win = pl.ds(pl.multiple_of(i * tq + ci * CHUNK, CHUNK), WBAND)
        return [_dot_nt(_stack2(qn_s[ci * CHUNK:(ci + 1) * CHUNK, sl], lane_lo), kp_s[win, sl])
                for sl in pair_sl]

    def finish(ci, sc):
        rows = slice(ci * CHUNK, (ci + 1) * CHUNK)
        q0 = i * tq + ci * CHUNK
        win = pl.ds(pl.multiple_of(q0, CHUNK), WBAND)
        valid = jpos >= WPAD - q0
        probs, invs = [], []
        for p in range(N_PAIRS):
            pes = []
            for e in range(2):
                se = sc[p][e * CHUNK:(e + 1) * CHUNK] + bias_ref[0, 2 * p + e]
                se = jnp.where(valid, se, MASK_VALUE)
                pe = jnp.exp2(se - jnp.max(se, axis=-1, keepdims=True))
                invs.append(1.0 / jnp.sum(pe, axis=-1, keepdims=True))
                pes.append(pe.astype(BF16))
            probs.append(jnp.concatenate(pes, axis=0))
        pv = [_dot(probs[p], vp_s[win, pair_sl[p]]) for p in range(N_PAIRS)]
        o = jnp.concatenate(
            [jnp.where(lane_lo, pv[p][0:CHUNK] * invs[2 * p], pv[p][CHUNK:2 * CHUNK] * invs[2 * p + 1])
             for p in range(N_PAIRS)], axis=1)
        g = g_ref[0, rows, :].astype(F32)
        o_ref[0, rows, :] = (o * (g * _sigmoid(g))).astype(BF16)

    sc_next = scores(0)
    for ci in range(n_chunks):
        sc = sc_next
        if ci + 1 < n_chunks:
            sc_next = scores(ci + 1)
        finish(ci, sc)


def _attn(att, q_g, k_g, bias_l, tq):
    b, s, _ = att.shape
    ones = jnp.asarray(
        (jnp.arange(PAIR)[:, None] // HEAD_DIM) == (jnp.arange(PAIR)[None, :] // HEAD_DIM), BF16)
    qg = jnp.tile(q_g, N_HEADS).reshape(1, D_ATT)
    kg = jnp.tile(k_g, N_HEADS).reshape(1, D_ATT)
    return pl.pallas_call(
        _attn_kernel,
        grid=(b, s // tq),
        in_specs=[
            pl.BlockSpec((1, tq, D_ATT), lambda bi, i: (bi, i, 0)),
            pl.BlockSpec((1, s, D_ATT), lambda bi, i: (bi, 0, 1)),
            pl.BlockSpec((1, s, D_ATT), lambda bi, i: (bi, 0, 2)),
            pl.BlockSpec((1, tq, D_ATT), lambda bi, i: (bi, i, 3)),
            pl.BlockSpec((1, D_ATT), lambda bi, i: (0, 0)),
            pl.BlockSpec((1, D_ATT), lambda bi, i: (0, 0)),
            pl.BlockSpec((PAIR, PAIR), lambda bi, i: (0, 0)),
            pl.BlockSpec((1, N_HEADS, CHUNK, WBAND), lambda bi, i: (0, 0, 0, 0)),
        ],
        out_specs=pl.BlockSpec((1, tq, D_ATT), lambda bi, i: (bi, i, 0)),
        out_shape=jax.ShapeDtypeStruct((b, s, D_ATT), BF16),
        scratch_shapes=[
            pltpu.VMEM((WPAD + s, D_ATT), BF16),
            pltpu.VMEM((WPAD + s, D_ATT), BF16),
            pltpu.VMEM((tq, D_ATT), BF16),
        ],
        compiler_params=_params(("arbitrary", "arbitrary")),
        name="band_attn",
    )(att, att, att, att, qg, kg, ones, bias_l)


def _outproj_kernel(x_ref, yr_ref, ya_ref, mod_ref, w_ref, o_ref):
    acc = _dot(yr_ref[0], w_ref[0:D_RWKV, :]) + _dot(ya_ref[0], w_ref[D_RWKV:D_RWKV + D_ATT, :])
    gate = mod_ref[0, :, 2 * D_MODEL:3 * D_MODEL]
    o_ref[0] = x_ref[0] + gate * acc


def _outproj(x, yr, ya, mod_l, w_out_all, layer, tm):
    b, s, d = x.shape
    return pl.pallas_call(
        _outproj_kernel,
        grid=(b, s // tm),
        in_specs=[
            pl.BlockSpec((1, tm, d), lambda bi, i: (bi, i, 0)),
            pl.BlockSpec((1, tm, D_RWKV), lambda bi, i: (bi, i, 0)),
            pl.BlockSpec((1, tm, D_ATT), lambda bi, i: (bi, i, 0)),
            pl.BlockSpec((1, 1, 3 * d), lambda bi, i: (bi, 0, 0)),
            pl.BlockSpec((None, D_RWKV + D_ATT, d), lambda bi, i: (layer, 0, 0)),
        ],
        out_specs=pl.BlockSpec((1, tm, d), lambda bi, i: (bi, i, 0)),
        out_shape=jax.ShapeDtypeStruct((b, s, d), F32),
        compiler_params=_params(("arbitrary", "arbitrary")),
        name="outproj",
    )(x, yr, ya, mod_l.reshape(b, 1, 3 * d), w_out_all)


def _tile(s, want):
    t = min(want, s)
    assert s % t == 0 and t % CHUNK == 0
    return t


def kernel(x, c, norm_g, w_ada, b_ada, w_in, mu_shift, w0, w2, a0, a2, k_k, k_a, r_k, lnx_g, lnx_b,
           q_norm_g, k_norm_g, rel_bias, w_out):
    n_layers = w_in.shape[0]
    s = x.shape[1]
    tm = _tile(s, 512)
    ts = _tile(s, 512)
    tq = _tile(s, 256)
    mod = _ada_mod(c, w_ada, b_ada)
    bias = _bias_tables(rel_bias)
    w_in_b = w_in.astype(BF16)
    w_out_b = w_out.astype(BF16)
    zeros = jnp.zeros((LORA, D_RWKV), F32)
    for l in range(n_layers):
        rkvg, wa, att = _inproj(x, mod[l], norm_g[l], w_in_b, l, mu_shift[l], tm)
        w2a2 = jnp.concatenate([jnp.concatenate([w2[l], zeros], axis=1),
                                jnp.concatenate([zeros, a2[l]], axis=1)], axis=0).astype(BF16)
        vecs = jnp.stack([w0[l], a0[l], k_k[l], k_a[l], r_k[l].reshape(-1), lnx_g[l], lnx_b[l],
                          jnp.zeros((D_RWKV,), F32)])
        yr = _rwkv(rkvg, wa, w2a2, vecs, ts)
        ya = _attn(att, q_norm_g[l], k_norm_g[l], bias[l:l + 1], tq)
        x = _outproj(x, yr, ya, mod[l], w_out_b, l, tm)
    return x
```

```python
import functools

import jax
import jax.numpy as jnp
from jax import lax
from jax.experimental import pallas as pl
from jax.experimental.pallas import tpu as pltpu

D_MODEL = 1024
CHUNK = 64
HEAD_DIM = 64
D_RWKV = 512
D_ATT = 512
N_HEADS = 8
LORA = 64
N_LEFT = 8
PAD = N_LEFT * CHUNK
BAND = PAD + CHUNK
WPAD = PAD + CHUNK
WBAND = WPAD + CHUNK
REL_CLIP = 128
N_REL = CHUNK + REL_CLIP
RMS_EPS = 1e-6
GN_EPS = 64e-5
D_SHIFT = 4 * D_RWKV + 2 * LORA
D_IN = D_SHIFT + 4 * D_ATT
PAIR = 2 * HEAD_DIM
N_PAIRS = N_HEADS // 2
QUAD = PAIR
N_QUADS = D_RWKV // QUAD
RWKV_CHUNKS_PER_STEP = 4
SUBLANES = 8
LOG2E = 1.4426950408889634
EXP_NEG_HALF = 0.6065306597126334
MASK_VALUE = -1e30

VMEM_LIMIT_V7X = 56 * 1024 * 1024

F32 = jnp.float32
BF16 = jnp.bfloat16


def _dot(a, b):
    return jnp.dot(a, b, preferred_element_type=F32)


def _dot_nt(a, b):
    return lax.dot_general(a, b, (((1,), (1,)), ((), ())), preferred_element_type=F32)


def _sigmoid(x):
    return 0.5 * jnp.tanh(0.5 * x) + 0.5


def _split2(x):
    hi = x.astype(BF16)
    lo = (x - hi.astype(F32)).astype(BF16)
    return hi, lo


def _split3(x):
    hi = x.astype(BF16)
    r1 = x - hi.astype(F32)
    mid = r1.astype(BF16)
    lo = (r1 - mid.astype(F32)).astype(BF16)
    return hi, mid, lo


def _params(sem):
    return pltpu.CompilerParams(dimension_semantics=sem, vmem_limit_bytes=VMEM_LIMIT_V7X)


def _ada_kernel(c_ref, w_ref, b_ref, o_ref):
    c = c_ref[...]
    ca = c / (1.0 + jnp.exp(-c))
    o_ref[0] = _dot(ca.astype(BF16), w_ref[0].astype(BF16)) + b_ref[0]


def _ada_mod(c, w_ada, b_ada):
    n_layers, d, d3 = w_ada.shape
    b = c.shape[0]
    tn = 1024
    return pl.pallas_call(
        _ada_kernel,
        grid=(n_layers, d3 // tn),
        in_specs=[
            pl.BlockSpec((b, d), lambda l, j: (0, 0)),
            pl.BlockSpec((1, d, tn), lambda l, j: (l, 0, j)),
            pl.BlockSpec((1, 1, tn), lambda l, j: (l, 0, j)),
        ],
        out_specs=pl.BlockSpec((1, b, tn), lambda l, j: (l, 0, j)),
        out_shape=jax.ShapeDtypeStruct((n_layers, b, d3), F32),
        compiler_params=_params(("arbitrary", "arbitrary")),
        name="ada_mod",
    )(c, w_ada, b_ada.reshape(n_layers, 1, d3))


def _bias_kernel(tbl_ref, o_ref, tab_s):
    l = pl.program_id(0)
    ji = lax.broadcasted_iota(jnp.int32, (SUBLANES, WBAND), 1)
    idx = jnp.clip(WPAD - ji, -(CHUNK - 1), REL_CLIP) + (CHUNK - 1)
    for p in range(N_PAIRS):
        for e in range(2):
            base = (l * N_HEADS + 2 * p + e) * N_REL

            def body(m, acc, base=base):
                return jnp.where(idx == m, tbl_ref[base + m], acc)

            row0 = lax.fori_loop(0, N_REL, body, jnp.zeros((SUBLANES, WBAND), F32)) * LOG2E
            for q in range(CHUNK):
                rolled = row0 if q == 0 else pltpu.roll(row0, q, 1)
                tab_s[e * CHUNK + q:e * CHUNK + q + 1, :] = jnp.where(
                    ji[0:1] < WPAD - PAD, MASK_VALUE, rolled[0:1])
        o_ref[0, p] = tab_s[...].T.astype(BF16)


def _bias_tables(rel_bias):
    n_layers = rel_bias.shape[0]
    return pl.pallas_call(
        _bias_kernel,
        grid=(n_layers,),
        in_specs=[pl.BlockSpec(memory_space=pltpu.SMEM)],
        out_specs=pl.BlockSpec((1, N_PAIRS, WBAND, PAIR), lambda l: (l, 0, 0, 0)),
        out_shape=jax.ShapeDtypeStruct((n_layers, N_PAIRS, WBAND, PAIR), BF16),
        scratch_shapes=[pltpu.VMEM((PAIR, WBAND), F32)],
        compiler_params=_params(("arbitrary",)),
        name="bias_tables",
    )(rel_bias.reshape(-1))


def _inproj_kernel(x_ref, mod_ref, g_ref, w_ref, mu_ref, rkvg_ref, wa_ref, att_ref, carry_ref):
    i = pl.program_id(1)
    tm = x_ref.shape[1]

    @pl.when(i == 0)
    def _():
        carry_ref[...] = jnp.zeros_like(carry_ref)

    x = x_ref[0]
    y = x * lax.rsqrt(jnp.mean(x * x, axis=-1, keepdims=True) + RMS_EPS)
    shift = mod_ref[0, :, 0:D_MODEL]
    scale = mod_ref[0, :, D_MODEL:2 * D_MODEL]
    h = ((y * g_ref[...]) * (1.0 + scale) + shift).astype(BF16)

    p = _dot(h, w_ref[:, 0:D_SHIFT])
    row = lax.broadcasted_iota(jnp.int32, (tm, D_SHIFT), 0)
    prev = jnp.where(row == 0, carry_ref[...], pltpu.roll(p, 1, 0))
    carry_ref[...] = p[tm - 1:tm, :]
    ps = p + mu_ref[...] * (prev - p)
    rkvg_ref[0] = ps[:, 0:4 * D_RWKV].astype(BF16)
    wa_ref[0] = ps[:, 4 * D_RWKV:D_SHIFT].astype(BF16)
    att_ref[0] = _dot(h, w_ref[:, D_SHIFT:D_IN]).astype(BF16)


def _inproj(x, mod_l, norm_g_l, w_in_all, layer, mu_l, tm):
    b, s, d = x.shape
    return pl.pallas_call(
        _inproj_kernel,
        grid=(b, s // tm),
        in_specs=[
            pl.BlockSpec((1, tm, d), lambda bi, i: (bi, i, 0)),
            pl.BlockSpec((1, 1, 3 * d), lambda bi, i: (bi, 0, 0)),
            pl.BlockSpec((1, d), lambda bi, i: (0, 0)),
            pl.BlockSpec((None, d, D_IN), lambda bi, i: (layer, 0, 0)),
            pl.BlockSpec((1, D_SHIFT), lambda bi, i: (0, 0)),
        ],
        out_specs=[
            pl.BlockSpec((1, tm, 4 * D_RWKV), lambda bi, i: (bi, i, 0)),
            pl.BlockSpec((1, tm, 2 * LORA), lambda bi, i: (bi, i, 0)),
            pl.BlockSpec((1, tm, 4 * D_ATT), lambda bi, i: (bi, i, 0)),
        ],
        out_shape=[
            jax.ShapeDtypeStruct((b, s, 4 * D_RWKV), BF16),
            jax.ShapeDtypeStruct((b, s, 2 * LORA), BF16),
            jax.ShapeDtypeStruct((b, s, 4 * D_ATT), BF16),
        ],
        scratch_shapes=[pltpu.VMEM((1, D_SHIFT), F32)],
        compiler_params=_params(("arbitrary", "arbitrary")),
        name="inproj",
    )(x, mod_l.reshape(b, 1, 3 * d), norm_g_l.reshape(1, d), w_in_all, mu_l.reshape(1, D_SHIFT))


def _stack2(x, lane_lo):
    zero = jnp.zeros_like(x)
    return jnp.concatenate([jnp.where(lane_lo, x, zero), jnp.where(lane_lo, zero, x)], axis=0)


def _rwkv_kernel(rkvg_ref, wa_ref, w2a2_ref, vec_ref, ones_ref, tri_ref, o_ref,
                 st_ref, lw_s, al_s, be_s, km_s, y_s):
    i = pl.program_id(1)
    ts = rkvg_ref.shape[1]
    n_chunks = ts // CHUNK

    @pl.when(i == 0)
    def _():
        st_ref[...] = jnp.zeros_like(st_ref)

    w0 = vec_ref[0:1, :]
    a0 = vec_ref[1:2, :]
    k_k = vec_ref[2:3, :]
    k_a = vec_ref[3:4, :]
    r_k = vec_ref[4:5, :]
    lnx_g = vec_ref[5:6, :]
    lnx_b = vec_ref[6:7, :]
    ones = ones_ref[...]

    def group_sum(x, exact=False):
        hi = x.astype(BF16)
        lo = (x - hi.astype(F32)).astype(BF16) if exact else None
        parts = []
        for p in range(N_PAIRS):
            sl = slice(p * PAIR, (p + 1) * PAIR)
            acc = _dot(hi[:, sl], ones)
            parts.append(acc + _dot(lo[:, sl], ones) if exact else acc)
        return jnp.concatenate(parts, axis=1)

    def prologue(rows):
        k = rkvg_ref[0, rows, D_RWKV:2 * D_RWKV].astype(F32)
        wa = wa_ref[0, rows, :].astype(F32)
        lane_wa = lax.broadcasted_iota(jnp.int32, wa.shape, 1)
        tw = jnp.where(lane_wa < LORA, jnp.tanh(wa), wa).astype(BF16)
        za = _dot(tw, w2a2_ref[...])
        z = w0 + za[:, 0:D_RWKV]
        aa = a0 + za[:, D_RWKV:2 * D_RWKV]
        lw_s[rows, :] = -EXP_NEG_HALF * _sigmoid(z)
        a = _sigmoid(aa)
        kk = k * k_k
        kk = kk * lax.rsqrt(jnp.maximum(group_sum(kk * kk, exact=True), 1e-24))
        al_s[rows, :] = -kk
        be_s[rows, :] = kk * a
        km_s[rows, :] = k * (1.0 + (a - 1.0) * k_a)

    tri = tri_ref[...]

    cat0 = functools.partial(jnp.concatenate, axis=0)
    cat1 = functools.partial(jnp.concatenate, axis=1)
    group = RWKV_CHUNKS_PER_STEP
    rowq = lax.broadcasted_iota(jnp.int32, (CHUNK, QUAD), 0)
    laneq = lax.broadcasted_iota(jnp.int32, (CHUNK, QUAD), 1)
    colq = laneq & (HEAD_DIM - 1)
    mq_strict = colq < rowq
    mq_incl = colq <= rowq
    eyeq = (colq == rowq).astype(F32)
    head_masks = [(laneq >= e * HEAD_DIM) & (laneq < (e + 1) * HEAD_DIM) for e in range(QUAD // HEAD_DIM)]

    def s4(x):
        zero = jnp.zeros_like(x)
        return cat0([jnp.where(m, x, zero) for m in head_masks])

    assert QUAD == PAIR and CHUNK == HEAD_DIM
    pair_lo = head_masks[0]
    pair_eye = colq == rowq
    pair_bd = s4

    def chains_part(gi):
        prep = []
        for ci in range(group):
            r0 = (gi * group + ci) * CHUNK
            rows = slice(r0, r0 + CHUNK)
            r = rkvg_ref[0, rows, 0:D_RWKV].astype(F32)
            v = rkvg_ref[0, rows, 2 * D_RWKV:3 * D_RWKV]
            lw = lw_s[rows, :]
            hi, lo = _split2(lw)
            cum = _dot(tri, hi) + _dot(tri, lo)
            w_inv = jnp.exp(-cum)
            w_all = jnp.exp(jnp.sum(lw, axis=0, keepdims=True))
            bt = be_s[rows, :] * w_inv
            kt = km_s[rows, :] * w_inv
            prep.append(dict(
                rows=rows, v=v,
                rt=r * jnp.exp(cum),
                at=(al_s[rows, :] * jnp.exp(cum - lw)).astype(BF16),
                bt=bt.astype(BF16),
                kt=kt.astype(BF16),
                bw=bt * w_all,
                kw=kt * w_all,
                w_all=w_all))
        chains = [(ci, q) for ci in range(group) for q in range(N_QUADS)]
        n_ch = len(chains)
        sls = [slice(q * QUAD, (q + 1) * QUAD) for _, q in chains]
        at_c = [prep[ci]["at"][:, sl] for (ci, _), sl in zip(chains, sls)]
        rt_c = [prep[ci]["rt"][:, sl] for (ci, _), sl in zip(chains, sls)]
        v_c = [prep[ci]["v"][:, sl] for (ci, _), sl in zip(chains, sls)]
        amat = [_dot_nt(cat0([at_c[j], rt_c[j].astype(BF16)]),
                        cat0([s4(prep[ci]["bt"][:, sls[j]]), s4(prep[ci]["kt"][:, sls[j]])]))
                for j, (ci, _) in enumerate(chains)]
        n1 = [jnp.where(mq_strict, a[0:CHUNK, 0:QUAD], 0.0) for a in amat]
        a_ak = [jnp.where(mq_strict, a[0:CHUNK, QUAD:2 * QUAD], 0.0).astype(BF16) for a in amat]
        a_rb = [jnp.where(mq_incl, a[CHUNK:2 * CHUNK, 0:QUAD], 0.0).astype(BF16) for a in amat]
        a_rk = [jnp.where(mq_incl, a[CHUNK:2 * CHUNK, QUAD:2 * QUAD], 0.0).astype(BF16) for a in amat]
        n1b = [n.astype(BF16) for n in n1]
        npow = [_dot(n, s4(n)) for n in n1b]
        av = [_dot(cat0([a_ak[j], a_rk[j]]), s4(v_c[j])) for j in range(n_ch)]
        tmat = [eyeq + n for n in n1]
        for _ in range(4):
            npb = [n.astype(BF16) for n in npow]
            both = [_dot(cat0([t.astype(BF16), n]), s4(n)) for t, n in zip(tmat, npb)]
            tmat = [t + b[0:CHUNK] for t, b in zip(tmat, both)]
            npow = [b[CHUNK:2 * CHUNK] for b in both]
        tmat = [t + _dot(t.astype(BF16), s4(n.astype(BF16))) for t, n in zip(tmat, npow)]
        t2 = [_dot(tmat[j].astype(BF16), cat1([s4(at_c[j]), s4(av[j][0:CHUNK].astype(BF16))]))
              for j in range(n_ch)]
        a_pr = [t[:, 0:QUAD].astype(BF16) for t in t2]
        uv = [t[:, QUAD:2 * QUAD].astype(BF16) for t in t2]
        r2 = [_dot(a_rb[j], cat1([s4(a_pr[j]), s4(uv[j])])) for j in range(n_ch)]
        r_pr = [(rt_c[j] + r2[j][:, 0:QUAD]).astype(BF16) for j in range(n_ch)]
        y0 = [av[j][CHUNK:2 * CHUNK] + r2[j][:, QUAD:2 * QUAD] for j in range(n_ch)]
        gmat, hmat = {}, {}
        for j, (ci, q) in enumerate(chains):
            for h in range(QUAD // PAIR):
                p = q * (QUAD // PAIR) + h
                psl = slice(p * PAIR, (p + 1) * PAIR)
                hsl = slice(h * PAIR, (h + 1) * PAIR)
                v_p = prep[ci]["v"][:, psl]
                lt = cat0([prep[ci]["bw"][:, psl], prep[ci]["kw"][:, psl]]).T.astype(BF16)
                rgt = cat0([cat1([a_pr[j][:, hsl], uv[j][:, hsl]]),
                            cat1([jnp.zeros_like(v_p), v_p])])
                gh = _dot(lt, rgt)
                g_sbs = jnp.where(pair_lo, gh[0:HEAD_DIM, 0:PAIR], gh[HEAD_DIM:PAIR, 0:PAIR])
                gmat[ci, p] = (g_sbs + jnp.where(pair_eye, prep[ci]["w_all"][:, psl], 0.0)).astype(BF16)
                hmat[ci, p] = jnp.where(pair_lo, gh[0:HEAD_DIM, PAIR:2 * PAIR], gh[HEAD_DIM:PAIR, PAIR:2 * PAIR])
        for ci in range(group):
            ys = []
            for p in range(N_PAIRS):
                j = ci * N_QUADS + p // (QUAD // PAIR)
                hsl = slice((p % (QUAD // PAIR)) * PAIR, (p % (QUAD // PAIR) + 1) * PAIR)
                seq = _dot(cat0([r_pr[j][:, hsl], gmat[ci, p]]), pair_bd(st_ref[p].astype(BF16)))
                ys.append(seq[0:CHUNK] + y0[j][:, hsl])
                st_ref[p] = seq[CHUNK:CHUNK + HEAD_DIM] + hmat[ci, p]
            y_s[prep[ci]["rows"], :] = cat1(ys)

    def epilogue(rows):
        y = y_s[rows, :]
        r = rkvg_ref[0, rows, 0:D_RWKV].astype(F32)
        v = rkvg_ref[0, rows, 2 * D_RWKV:3 * D_RWKV].astype(F32)
        g = rkvg_ref[0, rows, 3 * D_RWKV:4 * D_RWKV].astype(F32)
        inv_n = 1.0 / HEAD_DIM
        yc = y - group_sum(y, exact=True) * inv_n
        yn = yc * lax.rsqrt(group_sum(yc * yc) * inv_n + GN_EPS)
        out = yn * lnx_g + lnx_b + group_sum(r * km_s[rows, :] * r_k) * v
        o_ref[0, rows, :] = (out * (g * _sigmoid(g))).astype(BF16)

    part_rows = group * CHUNK
    n_parts = ts // part_rows
    parts = [slice(k * part_rows, (k + 1) * part_rows) for k in range(n_parts)]
    for k in range(n_parts + 2):
        if k < n_parts:
            prologue(parts[k])
        if 0 <= k - 1 < n_parts:
            chains_part(k - 1)
        if 0 <= k - 2 < n_parts:
            epilogue(parts[k - 2])


def _rwkv(rkvg, wa, w2a2, vecs, ts):
    b, s, _ = rkvg.shape
    ones = jnp.asarray(
        (jnp.arange(PAIR)[:, None] // HEAD_DIM) == (jnp.arange(PAIR)[None, :] // HEAD_DIM), BF16)
    tri = jnp.asarray(jnp.arange(CHUNK)[:, None] >= jnp.arange(CHUNK)[None, :], BF16)
    return pl.pallas_call(
        _rwkv_kernel,
        grid=(b, s // ts),
        in_specs=[
            pl.BlockSpec((1, ts, 4 * D_RWKV), lambda bi, i: (bi, i, 0)),
            pl.BlockSpec((1, ts, 2 * LORA), lambda bi, i: (bi, i, 0)),
            pl.BlockSpec((2 * LORA, 2 * D_RWKV), lambda bi, i: (0, 0)),
            pl.BlockSpec((8, D_RWKV), lambda bi, i: (0, 0)),
            pl.BlockSpec((PAIR, PAIR), lambda bi, i: (0, 0)),
            pl.BlockSpec((CHUNK, CHUNK), lambda bi, i: (0, 0)),
        ],
        out_specs=pl.BlockSpec((1, ts, D_RWKV), lambda bi, i: (bi, i, 0)),
        out_shape=jax.ShapeDtypeStruct((b, s, D_RWKV), BF16),
        scratch_shapes=[
            pltpu.VMEM((N_PAIRS, HEAD_DIM, PAIR), F32),
            pltpu.VMEM((ts, D_RWKV), F32),
            pltpu.VMEM((ts, D_RWKV), F32),
            pltpu.VMEM((ts, D_RWKV), F32),
            pltpu.VMEM((ts, D_RWKV), F32),
            pltpu.VMEM((ts, D_RWKV), F32),
        ],
        compiler_params=_params(("arbitrary", "arbitrary")),
        name="rwkv7",
    )(rkvg, wa, w2a2, vecs, ones, tri)


def _attn_kernel(q_ref, k_ref, v_ref, g_ref, qg_ref, kg_ref, ones_ref, bias_ref, o_ref,
                 kp_s, vp_s, qn_s):
    i = pl.program_id(1)
    tq = q_ref.shape[1]
    s = k_ref.shape[1]
    n_chunks = tq // CHUNK
    ones = ones_ref[...]

    def head_norm(x, gain):
        hi, lo = _split2(x * x)
        parts = []
        for p in range(N_PAIRS):
            sl = slice(p * PAIR, (p + 1) * PAIR)
            parts.append(_dot(hi[:, sl], ones) + _dot(lo[:, sl], ones))
        ms = jnp.concatenate(parts, axis=1) * (1.0 / HEAD_DIM)
        return (x * lax.rsqrt(ms + RMS_EPS) * gain).astype(BF16)

    @pl.when(i == 0)
    def _():
        kp_s[0:WPAD, :] = jnp.zeros((WPAD, D_ATT), BF16)
        vp_s[0:WPAD, :] = jnp.zeros((WPAD, D_ATT), BF16)
        kp_s[WPAD:WPAD + s, :] = head_norm(k_ref[0].astype(F32), kg_ref[...])
        vp_s[WPAD:WPAD + s, :] = v_ref[0]

    lane = lax.broadcasted_iota(jnp.int32, (CHUNK, PAIR), 1)
    lane_lo = lane < HEAD_DIM
    jpos = lax.broadcasted_iota(jnp.int32, (CHUNK, WBAND), 1)
    qn_s[...] = head_norm(q_ref[0].astype(F32), qg_ref[...] * (HEAD_DIM ** -0.5 * LOG2E))
    pair_sl = [slice(p * PAIR, (p + 1) * PAIR) for p in range(N_PAIRS)]
    rowi = lax.broadcasted_iota(jnp.int32, (PAIR, PAIR), 0)
    lanei = lax.broadcasted_iota(jnp.int32, (PAIR, PAIR), 1)
    eye = (rowi == lanei).astype(BF16)

    def scores(ci):
        win = pl.ds(pl.multiple_of(i * tq + ci * CHUNK, CHUNK), WBAND)
        return [_dot_nt(
            jnp.concatenate([_stack2(qn_s[ci * CHUNK:(ci + 1) * CHUNK, pair_sl[p]], lane_lo), eye], axis=1),
            jnp.concatenate([kp_s[win, pair_sl[p]], bias_ref[0, p]], axis=1))
            for p in range(N_PAIRS)]

    def finish(ci, sc, masked):
        rows = slice(ci * CHUNK, (ci + 1) * CHUNK)
        q0 = i * tq + ci * CHUNK
        win = pl.ds(pl.multiple_of(q0, CHUNK), WBAND)
        valid = jpos >= WPAD - q0
        probs, invs = [], []
        for p in range(N_PAIRS):
            pes = []
            for e in range(2):
                se = sc[p][e * CHUNK:(e + 1) * CHUNK]
                if masked:
                    se = jnp.where(valid, se, MASK_VALUE)
                pe = jnp.exp2(se - jnp.max(se, axis=-1, keepdims=True))
                invs.append(1.0 / jnp.sum(pe, axis=-1, keepdims=True))
                pes.append(pe.astype(BF16))
            probs.append(jnp.concatenate(pes, axis=0))
        pv = [_dot(probs[p], vp_s[win, pair_sl[p]]) for p in range(N_PAIRS)]
        o = jnp.concatenate(
            [jnp.where(lane_lo, pv[p][0:CHUNK] * invs[2 * p], pv[p][CHUNK:2 * CHUNK] * invs[2 * p + 1])
             for p in range(N_PAIRS)], axis=1)
        g = g_ref[0, rows, :].astype(F32)
        o_ref[0, rows, :] = (o * (g * _sigmoid(g))).astype(BF16)

    def run(masked):
        sc_next = scores(0)
        for ci in range(n_chunks):
            sc = sc_next
            if ci + 1 < n_chunks:
                sc_next = scores(ci + 1)
            finish(ci, sc, masked)

    needs_mask = i * tq < PAD

    @pl.when(needs_mask)
    def _():
        run(True)

    @pl.when(jnp.logical_not(needs_mask))
    def _():
        run(False)


def _attn(att, q_g, k_g, bias_all, layer, tq):
    b, s, _ = att.shape
    ones = jnp.asarray(
        (jnp.arange(PAIR)[:, None] // HEAD_DIM) == (jnp.arange(PAIR)[None, :] // HEAD_DIM), BF16)
    qg = jnp.tile(q_g, N_HEADS).reshape(1, D_ATT)
    kg = jnp.tile(k_g, N_HEADS).reshape(1, D_ATT)
    return pl.pallas_call(
        _attn_kernel,
        grid=(b, s // tq),
        in_specs=[
            pl.BlockSpec((1, tq, D_ATT), lambda bi, i: (bi, i, 0)),
            pl.BlockSpec((1, s, D_ATT), lambda bi, i: (bi, 0, 1)),
            pl.BlockSpec((1, s, D_ATT), lambda bi, i: (bi, 0, 2)),
            pl.BlockSpec((1, tq, D_ATT), lambda bi, i: (bi, i, 3)),
            pl.BlockSpec((1, D_ATT), lambda bi, i: (0, 0)),
            pl.BlockSpec((1, D_ATT), lambda bi, i: (0, 0)),
            pl.BlockSpec((PAIR, PAIR), lambda bi, i: (0, 0)),
            pl.BlockSpec((1, N_PAIRS, WBAND, PAIR), lambda bi, i: (layer, 0, 0, 0)),
        ],
        out_specs=pl.BlockSpec((1, tq, D_ATT), lambda bi, i: (bi, i, 0)),
        out_shape=jax.ShapeDtypeStruct((b, s, D_ATT), BF16),
        scratch_shapes=[
            pltpu.VMEM((WPAD + s, D_ATT), BF16),
            pltpu.VMEM((WPAD + s, D_ATT), BF16),
            pltpu.VMEM((tq, D_ATT), BF16),
        ],
        compiler_params=_params(("arbitrary", "arbitrary")),
        name="band_attn",
    )(att, att, att, att, qg, kg, ones, bias_all)


def _outproj_kernel(x_ref, yr_ref, ya_ref, mod_ref, w_ref, o_ref):
    acc = _dot(yr_ref[0], w_ref[0:D_RWKV, :]) + _dot(ya_ref[0], w_ref[D_RWKV:D_RWKV + D_ATT, :])
    gate = mod_ref[0, :, 2 * D_MODEL:3 * D_MODEL]
    o_ref[0] = x_ref[0] + gate * acc


def _outproj(x, yr, ya, mod_l, w_out_all, layer, tm):
    b, s, d = x.shape
    return pl.pallas_call(
        _outproj_kernel,
        grid=(b, s // tm),
        in_specs=[
            pl.BlockSpec((1, tm, d), lambda bi, i: (bi, i, 0)),
            pl.BlockSpec((1, tm, D_RWKV), lambda bi, i: (bi, i, 0)),
            pl.BlockSpec((1, tm, D_ATT), lambda bi, i: (bi, i, 0)),
            pl.BlockSpec((1, 1, 3 * d), lambda bi, i: (bi, 0, 0)),
            pl.BlockSpec((None, D_RWKV + D_ATT, d), lambda bi, i: (layer, 0, 0)),
        ],
        out_specs=pl.BlockSpec((1, tm, d), lambda bi, i: (bi, i, 0)),
        out_shape=jax.ShapeDtypeStruct((b, s, d), F32),
        compiler_params=_params(("arbitrary", "arbitrary")),
        name="outproj",
    )(x, yr, ya, mod_l.reshape(b, 1, 3 * d), w_out_all)


def _tile(s, want):
    t = min(want, s)
    assert s % t == 0 and t % CHUNK == 0
    return t


def kernel(x, c, norm_g, w_ada, b_ada, w_in, mu_shift, w0, w2, a0, a2, k_k, k_a, r_k, lnx_g, lnx_b,
           q_norm_g, k_norm_g, rel_bias, w_out):
    n_layers = w_in.shape[0]
    s = x.shape[1]
    tm = _tile(s, 512)
    ts = _tile(s, 512)
    tq = _tile(s, 256)
    mod = _ada_mod(c, w_ada, b_ada)
    bias = _bias_tables(rel_bias)
    w_in_b = w_in.astype(BF16)
    w_out_b = w_out.astype(BF16)
    zeros = jnp.zeros((LORA, D_RWKV), F32)
    for l in range(n_layers):
        rkvg, wa, att = _inproj(x, mod[l], norm_g[l], w_in_b, l, mu_shift[l], tm)
        w2a2 = jnp.concatenate([jnp.concatenate([w2[l], zeros], axis=1),
                                jnp.concatenate([zeros, a2[l]], axis=1)], axis=0).astype(BF16)
        vecs = jnp.stack([w0[l], a0[l], k_k[l], k_a[l], r_k[l].reshape(-1), lnx_g[l], lnx_b[l],
                          jnp.zeros((D_RWKV,), F32)])
        yr = _rwkv(rkvg, wa, w2a2, vecs, ts)
        ya = _attn(att, q_norm_g[l], k_norm_g[l], bias, l, tq)
        x = _outproj(x, yr, ya, mod[l], w_out_b, l, tm)
    return x
```

```python
import functools

import jax
import jax.numpy as jnp
from jax import lax
from jax.experimental import pallas as pl
from jax.experimental.pallas import tpu as pltpu

D_MODEL = 1024
CHUNK = 64
HEAD_DIM = 64
D_RWKV = 512
D_ATT = 512
N_HEADS = 8
LORA = 64
N_LEFT = 8
PAD = N_LEFT * CHUNK
BAND = PAD + CHUNK
WPAD = PAD + CHUNK
WBAND = WPAD + CHUNK
REL_CLIP = 128
N_REL = CHUNK + REL_CLIP
RMS_EPS = 1e-6
GN_EPS = 64e-5
D_SHIFT = 4 * D_RWKV + 2 * LORA
D_IN = D_SHIFT + 4 * D_ATT
PAIR = 2 * HEAD_DIM
N_PAIRS = N_HEADS // 2
QUAD = PAIR
N_QUADS = D_RWKV // QUAD
RWKV_CHUNKS_PER_STEP = 4
SUBLANES = 8
LOG2E = 1.4426950408889634
EXP_NEG_HALF = 0.6065306597126334
MASK_VALUE = -1e30

VMEM_LIMIT_V7X = 56 * 1024 * 1024

F32 = jnp.float32
BF16 = jnp.bfloat16


def _dot(a, b):
    return jnp.dot(a, b, preferred_element_type=F32)


def _dot_nt(a, b):
    return lax.dot_general(a, b, (((1,), (1,)), ((), ())), preferred_element_type=F32)


def _sigmoid(x):
    return 0.5 * jnp.tanh(0.5 * x) + 0.5


def _split2(x):
    hi = x.astype(BF16)
    lo = (x - hi.astype(F32)).astype(BF16)
    return hi, lo


def _split3(x):
    hi = x.astype(BF16)
    r1 = x - hi.astype(F32)
    mid = r1.astype(BF16)
    lo = (r1 - mid.astype(F32)).astype(BF16)
    return hi, mid, lo


def _params(sem):
    return pltpu.CompilerParams(dimension_semantics=sem, vmem_limit_bytes=VMEM_LIMIT_V7X)


def _ada_kernel(c_ref, w_ref, b_ref, o_ref):
    c = c_ref[...]
    ca = c / (1.0 + jnp.exp(-c))
    o_ref[0] = _dot(ca.astype(BF16), w_ref[0].astype(BF16)) + b_ref[0]


def _ada_mod(c, w_ada, b_ada):
    n_layers, d, d3 = w_ada.shape
    b = c.shape[0]
    tn = 1024
    return pl.pallas_call(
        _ada_kernel,
        grid=(n_layers, d3 // tn),
        in_specs=[
            pl.BlockSpec((b, d), lambda l, j: (0, 0)),
            pl.BlockSpec((1, d, tn), lambda l, j: (l, 0, j)),
            pl.BlockSpec((1, 1, tn), lambda l, j: (l, 0, j)),
        ],
        out_specs=pl.BlockSpec((1, b, tn), lambda l, j: (l, 0, j)),
        out_shape=jax.ShapeDtypeStruct((n_layers, b, d3), F32),
        compiler_params=_params(("arbitrary", "arbitrary")),
        name="ada_mod",
    )(c, w_ada, b_ada.reshape(n_layers, 1, d3))


def _bias_kernel(tbl_ref, o_ref):
    l = pl.program_id(0)
    ji = lax.broadcasted_iota(jnp.int32, (SUBLANES, WBAND), 1)
    idx = jnp.clip(WPAD - ji, -(CHUNK - 1), REL_CLIP) + (CHUNK - 1)
    for h in range(N_HEADS):
        base = (l * N_HEADS + h) * N_REL

        def body(m, acc, base=base):
            return jnp.where(idx == m, tbl_ref[base + m], acc)

        row0 = lax.fori_loop(0, N_REL, body, jnp.zeros((SUBLANES, WBAND), F32)) * LOG2E
        for q in range(CHUNK):
            rolled = row0 if q == 0 else pltpu.roll(row0, q, 1)
            o_ref[0, h, q:q + 1, :] = jnp.where(ji[0:1] < WPAD - PAD, MASK_VALUE, rolled[0:1])


def _bias_tables(rel_bias):
    n_layers = rel_bias.shape[0]
    return pl.pallas_call(
        _bias_kernel,
        grid=(n_layers,),
        in_specs=[pl.BlockSpec(memory_space=pltpu.SMEM)],
        out_specs=pl.BlockSpec((1, N_HEADS, CHUNK, WBAND), lambda l: (l, 0, 0, 0)),
        out_shape=jax.ShapeDtypeStruct((n_layers, N_HEADS, CHUNK, WBAND), F32),
        compiler_params=_params(("arbitrary",)),
        name="bias_tables",
    )(rel_bias.reshape(-1))


def _inproj_kernel(x_ref, mod_ref, g_ref, w_ref, mu_ref, rkvg_ref, wa_ref, att_ref, carry_ref):
    i = pl.program_id(1)
    tm = x_ref.shape[1]

    @pl.when(i == 0)
    def _():
        carry_ref[...] = jnp.zeros_like(carry_ref)

    x = x_ref[0]
    y = x * lax.rsqrt(jnp.mean(x * x, axis=-1, keepdims=True) + RMS_EPS)
    shift = mod_ref[0, :, 0:D_MODEL]
    scale = mod_ref[0, :, D_MODEL:2 * D_MODEL]
    h = ((y * g_ref[...]) * (1.0 + scale) + shift).astype(BF16)

    p = _dot(h, w_ref[:, 0:D_SHIFT])
    row = lax.broadcasted_iota(jnp.int32, (tm, D_SHIFT), 0)
    prev = jnp.where(row == 0, carry_ref[...], pltpu.roll(p, 1, 0))
    carry_ref[...] = p[tm - 1:tm, :]
    ps = p + mu_ref[...] * (prev - p)
    rkvg_ref[0] = ps[:, 0:4 * D_RWKV].astype(BF16)
    wa_ref[0] = ps[:, 4 * D_RWKV:D_SHIFT].astype(BF16)
    att_ref[0] = _dot(h, w_ref[:, D_SHIFT:D_IN]).astype(BF16)


def _inproj(x, mod_l, norm_g_l, w_in_all, layer, mu_l, tm):
    b, s, d = x.shape
    return pl.pallas_call(
        _inproj_kernel,
        grid=(b, s // tm),
        in_specs=[
            pl.BlockSpec((1, tm, d), lambda bi, i: (bi, i, 0)),
            pl.BlockSpec((1, 1, 3 * d), lambda bi, i: (bi, 0, 0)),
            pl.BlockSpec((1, d), lambda bi, i: (0, 0)),
            pl.BlockSpec((None, d, D_IN), lambda bi, i: (layer, 0, 0)),
            pl.BlockSpec((1, D_SHIFT), lambda bi, i: (0, 0)),
        ],
        out_specs=[
            pl.BlockSpec((1, tm, 4 * D_RWKV), lambda bi, i: (bi, i, 0)),
            pl.BlockSpec((1, tm, 2 * LORA), lambda bi, i: (bi, i, 0)),
            pl.BlockSpec((1, tm, 4 * D_ATT), lambda bi, i: (bi, i, 0)),
        ],
        out_shape=[
            jax.ShapeDtypeStruct((b, s, 4 * D_RWKV), BF16),
            jax.ShapeDtypeStruct((b, s, 2 * LORA), BF16),
            jax.ShapeDtypeStruct((b, s, 4 * D_ATT), BF16),
        ],
        scratch_shapes=[pltpu.VMEM((1, D_SHIFT), F32)],
        compiler_params=_params(("arbitrary", "arbitrary")),
        name="inproj",
    )(x, mod_l.reshape(b, 1, 3 * d), norm_g_l.reshape(1, d), w_in_all, mu_l.reshape(1, D_SHIFT))


def _stack2(x, lane_lo):
    zero = jnp.zeros_like(x)
    return jnp.concatenate([jnp.where(lane_lo, x, zero), jnp.where(lane_lo, zero, x)], axis=0)


def _rwkv_kernel(rkvg_ref, wa_ref, w2a2_ref, vec_ref, ones_ref, tri_ref, o_ref,
                 st_ref, lw_s, al_s, be_s, km_s, y_s):
    i = pl.program_id(1)
    ts = rkvg_ref.shape[1]
    n_chunks = ts // CHUNK

    @pl.when(i == 0)
    def _():
        st_ref[...] = jnp.zeros_like(st_ref)

    w0 = vec_ref[0:1, :]
    a0 = vec_ref[1:2, :]
    k_k = vec_ref[2:3, :]
    k_a = vec_ref[3:4, :]
    r_k = vec_ref[4:5, :]
    lnx_g = vec_ref[5:6, :]
    lnx_b = vec_ref[6:7, :]
    ones = ones_ref[...]

    def group_sum(x, exact=False):
        hi = x.astype(BF16)
        lo = (x - hi.astype(F32)).astype(BF16) if exact else None
        parts = []
        for p in range(N_PAIRS):
            sl = slice(p * PAIR, (p + 1) * PAIR)
            acc = _dot(hi[:, sl], ones)
            parts.append(acc + _dot(lo[:, sl], ones) if exact else acc)
        return jnp.concatenate(parts, axis=1)

    def prologue(rows):
        k = rkvg_ref[0, rows, D_RWKV:2 * D_RWKV].astype(F32)
        wa = wa_ref[0, rows, :].astype(F32)
        lane_wa = lax.broadcasted_iota(jnp.int32, wa.shape, 1)
        tw = jnp.where(lane_wa < LORA, jnp.tanh(wa), wa).astype(BF16)
        za = _dot(tw, w2a2_ref[...])
        z = w0 + za[:, 0:D_RWKV]
        aa = a0 + za[:, D_RWKV:2 * D_RWKV]
        lw_s[rows, :] = -EXP_NEG_HALF * _sigmoid(z)
        a = _sigmoid(aa)
        kk = k * k_k
        kk = kk * lax.rsqrt(jnp.maximum(group_sum(kk * kk, exact=True), 1e-24))
        al_s[rows, :] = -kk
        be_s[rows, :] = kk * a
        km_s[rows, :] = k * (1.0 + (a - 1.0) * k_a)

    tri = tri_ref[...]

    cat0 = functools.partial(jnp.concatenate, axis=0)
    cat1 = functools.partial(jnp.concatenate, axis=1)
    group = RWKV_CHUNKS_PER_STEP
    rowq = lax.broadcasted_iota(jnp.int32, (CHUNK, QUAD), 0)
    laneq = lax.broadcasted_iota(jnp.int32, (CHUNK, QUAD), 1)
    colq = laneq & (HEAD_DIM - 1)
    mq_strict = colq < rowq
    mq_incl = colq <= rowq
    eyeq = (colq == rowq).astype(F32)
    head_masks = [(laneq >= e * HEAD_DIM) & (laneq < (e + 1) * HEAD_DIM) for e in range(QUAD // HEAD_DIM)]

    def s4(x):
        zero = jnp.zeros_like(x)
        return cat0([jnp.where(m, x, zero) for m in head_masks])

    assert QUAD == PAIR and CHUNK == HEAD_DIM
    pair_lo = head_masks[0]
    pair_eye = colq == rowq
    pair_bd = s4

    def chains_part(gi):
        prep = []
        for ci in range(group):
            r0 = (gi * group + ci) * CHUNK
            rows = slice(r0, r0 + CHUNK)
            r = rkvg_ref[0, rows, 0:D_RWKV].astype(F32)
            v = rkvg_ref[0, rows, 2 * D_RWKV:3 * D_RWKV]
            lw = lw_s[rows, :]
            hi, lo = _split2(lw)
            cum = _dot(tri, hi) + _dot(tri, lo)
            w_inv = jnp.exp(-cum)
            w_all = jnp.exp(jnp.sum(lw, axis=0, keepdims=True))
            bt = be_s[rows, :] * w_inv
            kt = km_s[rows, :] * w_inv
            prep.append(dict(
                rows=rows, v=v,
                rt=r * jnp.exp(cum),
                at=(al_s[rows, :] * jnp.exp(cum - lw)).astype(BF16),
                bt=bt.astype(BF16),
                kt=kt.astype(BF16),
                bw=bt * w_all,
                kw=kt * w_all,
                w_all=w_all))
        chains = [(ci, q) for ci in range(group) for q in range(N_QUADS)]
        n_ch = len(chains)
        sls = [slice(q * QUAD, (q + 1) * QUAD) for _, q in chains]
        at_c = [prep[ci]["at"][:, sl] for (ci, _), sl in zip(chains, sls)]
        rt_c = [prep[ci]["rt"][:, sl] for (ci, _), sl in zip(chains, sls)]
        v_c = [prep[ci]["v"][:, sl] for (ci, _), sl in zip(chains, sls)]
        amat = [_dot_nt(cat0([at_c[j], rt_c[j].astype(BF16)]),
                        cat0([s4(prep[ci]["bt"][:, sls[j]]), s4(prep[ci]["kt"][:, sls[j]])]))
                for j, (ci, _) in enumerate(chains)]
        n1 = [jnp.where(mq_strict, a[0:CHUNK, 0:QUAD], 0.0) for a in amat]
        a_ak = [jnp.where(mq_strict, a[0:CHUNK, QUAD:2 * QUAD], 0.0).astype(BF16) for a in amat]
        a_rb = [jnp.where(mq_incl, a[CHUNK:2 * CHUNK, 0:QUAD], 0.0).astype(BF16) for a in amat]
        a_rk = [jnp.where(mq_incl, a[CHUNK:2 * CHUNK, QUAD:2 * QUAD], 0.0).astype(BF16) for a in amat]
        n1b = [n.astype(BF16) for n in n1]
        npow = [_dot(n, s4(n)) for n in n1b]
        av = [_dot(cat0([a_ak[j], a_rk[j]]), s4(v_c[j])) for j in range(n_ch)]
        tmat = [eyeq + n for n in n1]
        for _ in range(4):
            npb = [n.astype(BF16) for n in npow]
            both = [_dot(cat0([t.astype(BF16), n]), s4(n)) for t, n in zip(tmat, npb)]
            tmat = [t + b[0:CHUNK] for t, b in zip(tmat, both)]
            npow = [b[CHUNK:2 * CHUNK] for b in both]
        tmat = [t + _dot(t.astype(BF16), s4(n.astype(BF16))) for t, n in zip(tmat, npow)]
        t2 = [_dot(tmat[j].astype(BF16), cat1([s4(at_c[j]), s4(av[j][0:CHUNK].astype(BF16))]))
              for j in range(n_ch)]
        a_pr = [t[:, 0:QUAD].astype(BF16) for t in t2]
        uv = [t[:, QUAD:2 * QUAD].astype(BF16) for t in t2]
        r2 = [_dot(a_rb[j], cat1([s4(a_pr[j]), s4(uv[j])])) for j in range(n_ch)]
        r_pr = [(rt_c[j] + r2[j][:, 0:QUAD]).astype(BF16) for j in range(n_ch)]
        y0 = [av[j][CHUNK:2 * CHUNK] + r2[j][:, QUAD:2 * QUAD] for j in range(n_ch)]
        gmat, hmat = {}, {}
        for j, (ci, q) in enumerate(chains):
            for h in range(QUAD // PAIR):
                p = q * (QUAD // PAIR) + h
                psl = slice(p * PAIR, (p + 1) * PAIR)
                hsl = slice(h * PAIR, (h + 1) * PAIR)
                v_p = prep[ci]["v"][:, psl]
                lt = cat0([prep[ci]["bw"][:, psl], prep[ci]["kw"][:, psl]]).T.astype(BF16)
                rgt = cat0([cat1([a_pr[j][:, hsl], uv[j][:, hsl]]),
                            cat1([jnp.zeros_like(v_p), v_p])])
                gh = _dot(lt, rgt)
                g_sbs = jnp.where(pair_lo, gh[0:HEAD_DIM, 0:PAIR], gh[HEAD_DIM:PAIR, 0:PAIR])
                gmat[ci, p] = (g_sbs + jnp.where(pair_eye, prep[ci]["w_all"][:, psl], 0.0)).astype(BF16)
                hmat[ci, p] = jnp.where(pair_lo, gh[0:HEAD_DIM, PAIR:2 * PAIR], gh[HEAD_DIM:PAIR, PAIR:2 * PAIR])
        for ci in range(group):
            ys = []
            for p in range(N_PAIRS):
                j = ci * N_QUADS + p // (QUAD // PAIR)
                hsl = slice((p % (QUAD // PAIR)) * PAIR, (p % (QUAD // PAIR) + 1) * PAIR)
                seq = _dot(cat0([r_pr[j][:, hsl], gmat[ci, p]]), pair_bd(st_ref[p].astype(BF16)))
                ys.append(seq[0:CHUNK] + y0[j][:, hsl])
                st_ref[p] = seq[CHUNK:CHUNK + HEAD_DIM] + hmat[ci, p]
            y_s[prep[ci]["rows"], :] = cat1(ys)

    def epilogue(rows):
        y = y_s[rows, :]
        r = rkvg_ref[0, rows, 0:D_RWKV].astype(F32)
        v = rkvg_ref[0, rows, 2 * D_RWKV:3 * D_RWKV].astype(F32)
        g = rkvg_ref[0, rows, 3 * D_RWKV:4 * D_RWKV].astype(F32)
        inv_n = 1.0 / HEAD_DIM
        yc = y - group_sum(y, exact=True) * inv_n
        yn = yc * lax.rsqrt(group_sum(yc * yc) * inv_n + GN_EPS)
        out = yn * lnx_g + lnx_b + group_sum(r * km_s[rows, :] * r_k) * v
        o_ref[0, rows, :] = (out * (g * _sigmoid(g))).astype(BF16)

    part_rows = group * CHUNK
    n_parts = ts // part_rows
    parts = [slice(k * part_rows, (k + 1) * part_rows) for k in range(n_parts)]
    for k in range(n_parts + 2):
        if k < n_parts:
            prologue(parts[k])
        if 0 <= k - 1 < n_parts:
            chains_part(k - 1)
        if 0 <= k - 2 < n_parts:
            epilogue(parts[k - 2])


def _rwkv(rkvg, wa, w2a2, vecs, ts):
    b, s, _ = rkvg.shape
    ones = jnp.asarray(
        (jnp.arange(PAIR)[:, None] // HEAD_DIM) == (jnp.arange(PAIR)[None, :] // HEAD_DIM), BF16)
    tri = jnp.asarray(jnp.arange(CHUNK)[:, None] >= jnp.arange(CHUNK)[None, :], BF16)
    return pl.pallas_call(
        _rwkv_kernel,
        grid=(b, s // ts),
        in_specs=[
            pl.BlockSpec((1, ts, 4 * D_RWKV), lambda bi, i: (bi, i, 0)),
            pl.BlockSpec((1, ts, 2 * LORA), lambda bi, i: (bi, i, 0)),
            pl.BlockSpec((2 * LORA, 2 * D_RWKV), lambda bi, i: (0, 0)),
            pl.BlockSpec((8, D_RWKV), lambda bi, i: (0, 0)),
            pl.BlockSpec((PAIR, PAIR), lambda bi, i: (0, 0)),
            pl.BlockSpec((CHUNK, CHUNK), lambda bi, i: (0, 0)),
        ],
        out_specs=pl.BlockSpec((1, ts, D_RWKV), lambda bi, i: (bi, i, 0)),
        out_shape=jax.ShapeDtypeStruct((b, s, D_RWKV), BF16),
        scratch_shapes=[
            pltpu.VMEM((N_PAIRS, HEAD_DIM, PAIR), F32),
            pltpu.VMEM((ts, D_RWKV), F32),
            pltpu.VMEM((ts, D_RWKV), F32),
            pltpu.VMEM((ts, D_RWKV), F32),
            pltpu.VMEM((ts, D_RWKV), F32),
            pltpu.VMEM((ts, D_RWKV), F32),
        ],
        compiler_params=_params(("arbitrary", "arbitrary")),
        name="rwkv7",
    )(rkvg, wa, w2a2, vecs, ones, tri)


def _attn_kernel(q_ref, k_ref, v_ref, g_ref, qg_ref, kg_ref, ones_ref, bias_ref, o_ref,
                 kp_s, vp_s, qn_s):
    i = pl.program_id(1)
    tq = q_ref.shape[1]
    s = k_ref.shape[1]
    n_chunks = tq // CHUNK
    ones = ones_ref[...]

    def head_norm(x, gain):
        hi, lo = _split2(x * x)
        parts = []
        for p in range(N_PAIRS):
            sl = slice(p * PAIR, (p + 1) * PAIR)
            parts.append(_dot(hi[:, sl], ones) + _dot(lo[:, sl], ones))
        ms = jnp.concatenate(parts, axis=1) * (1.0 / HEAD_DIM)
        return (x * lax.rsqrt(ms + RMS_EPS) * gain).astype(BF16)

    @pl.when(i == 0)
    def _():
        kp_s[0:WPAD, :] = jnp.zeros((WPAD, D_ATT), BF16)
        vp_s[0:WPAD, :] = jnp.zeros((WPAD, D_ATT), BF16)
        kp_s[WPAD:WPAD + s, :] = head_norm(k_ref[0].astype(F32), kg_ref[...])
        vp_s[WPAD:WPAD + s, :] = v_ref[0]

    lane = lax.broadcasted_iota(jnp.int32, (CHUNK, PAIR), 1)
    lane_lo = lane < HEAD_DIM
    jpos = lax.broadcasted_iota(jnp.int32, (CHUNK, WBAND), 1)
    qn_s[...] = head_norm(q_ref[0].astype(F32), qg_ref[...] * (HEAD_DIM ** -0.5 * LOG2E))
    pair_sl = [slice(p * PAIR, (p + 1) * PAIR) for p in range(N_PAIRS)]

    def scores(ci):
        win = pl.ds(pl.multiple_of(i * tq + ci * CHUNK, CHUNK), WBAND)
        return [_dot_nt(_stack2(qn_s[ci * CHUNK:(ci + 1) * CHUNK, sl], lane_lo), kp_s[win, sl])
                for sl in pair_sl]

    def finish(ci, sc, masked):
        rows = slice(ci * CHUNK, (ci + 1) * CHUNK)
        q0 = i * tq + ci * CHUNK
        win = pl.ds(pl.multiple_of(q0, CHUNK), WBAND)
        valid = jpos >= WPAD - q0
        probs, invs = [], []
        for p in range(N_PAIRS):
            pes = []
            for e in range(2):
                se = sc[p][e * CHUNK:(e + 1) * CHUNK] + bias_ref[0, 2 * p + e]
                if masked:
                    se = jnp.where(valid, se, MASK_VALUE)
                pe = jnp.exp2(se - jnp.max(se, axis=-1, keepdims=True))
                invs.append(1.0 / jnp.sum(pe, axis=-1, keepdims=True))
                pes.append(pe.astype(BF16))
            probs.append(jnp.concatenate(pes, axis=0))
        pv = [_dot(probs[p], vp_s[win, pair_sl[p]]) for p in range(N_PAIRS)]
        o = jnp.concatenate(
            [jnp.where(lane_lo, pv[p][0:CHUNK] * invs[2 * p], pv[p][CHUNK:2 * CHUNK] * invs[2 * p + 1])
             for p in range(N_PAIRS)], axis=1)
        g = g_ref[0, rows, :].astype(F32)
        o_ref[0, rows, :] = (o * (g * _sigmoid(g))).astype(BF16)

    def run(masked):
        sc_next = scores(0)
        for ci in range(n_chunks):
            sc = sc_next
            if ci + 1 < n_chunks:
                sc_next = scores(ci + 1)
            finish(ci, sc, masked)

    needs_mask = i * tq < PAD

    @pl.when(needs_mask)
    def _():
        run(True)

    @pl.when(jnp.logical_not(needs_mask))
    def _():
        run(False)


def _attn(att, q_g, k_g, bias_all, layer, tq):
    b, s, _ = att.shape
    ones = jnp.asarray(
        (jnp.arange(PAIR)[:, None] // HEAD_DIM) == (jnp.arange(PAIR)[None, :] // HEAD_DIM), BF16)
    qg = jnp.tile(q_g, N_HEADS).reshape(1, D_ATT)
    kg = jnp.tile(k_g, N_HEADS).reshape(1, D_ATT)
    return pl.pallas_call(
        _attn_kernel,
        grid=(b, s // tq),
        in_specs=[
            pl.BlockSpec((1, tq, D_ATT), lambda bi, i: (bi, i, 0)),
            pl.BlockSpec((1, s, D_ATT), lambda bi, i: (bi, 0, 1)),
            pl.BlockSpec((1, s, D_ATT), lambda bi, i: (bi, 0, 2)),
            pl.BlockSpec((1, tq, D_ATT), lambda bi, i: (bi, i, 3)),
            pl.BlockSpec((1, D_ATT), lambda bi, i: (0, 0)),
            pl.BlockSpec((1, D_ATT), lambda bi, i: (0, 0)),
            pl.BlockSpec((PAIR, PAIR), lambda bi, i: (0, 0)),
            pl.BlockSpec((1, N_HEADS, CHUNK, WBAND), lambda bi, i: (layer, 0, 0, 0)),
        ],
        out_specs=pl.BlockSpec((1, tq, D_ATT), lambda bi, i: (bi, i, 0)),
        out_shape=jax.ShapeDtypeStruct((b, s, D_ATT), BF16),
        scratch_shapes=[
            pltpu.VMEM((WPAD + s, D_ATT), BF16),
            pltpu.VMEM((WPAD + s, D_ATT), BF16),
            pltpu.VMEM((tq, D_ATT), BF16),
        ],
        compiler_params=_params(("arbitrary", "arbitrary")),
        name="band_attn",
    )(att, att, att, att, qg, kg, ones, bias_all)


def _outproj_kernel(x_ref, yr_ref, ya_ref, mod_ref, w_ref, o_ref):
    acc = _dot(yr_ref[0], w_ref[0:D_RWKV, :]) + _dot(ya_ref[0], w_ref[D_RWKV:D_RWKV + D_ATT, :])
    gate = mod_ref[0, :, 2 * D_MODEL:3 * D_MODEL]
    o_ref[0] = x_ref[0] + gate * acc


def _outproj(x, yr, ya, mod_l, w_out_all, layer, tm):
    b, s, d = x.shape
    return pl.pallas_call(
        _outproj_kernel,
        grid=(b, s // tm),
        in_specs=[
            pl.BlockSpec((1, tm, d), lambda bi, i: (bi, i, 0)),
            pl.BlockSpec((1, tm, D_RWKV), lambda bi, i: (bi, i, 0)),
            pl.BlockSpec((1, tm, D_ATT), lambda bi, i: (bi, i, 0)),
            pl.BlockSpec((1, 1, 3 * d), lambda bi, i: (bi, 0, 0)),
            pl.BlockSpec((None, D_RWKV + D_ATT, d), lambda bi, i: (layer, 0, 0)),
        ],
        out_specs=pl.BlockSpec((1, tm, d), lambda bi, i: (bi, i, 0)),
        out_shape=jax.ShapeDtypeStruct((b, s, d), F32),
        compiler_params=_params(("arbitrary", "arbitrary")),
        name="outproj",
    )(x, yr, ya, mod_l.reshape(b, 1, 3 * d), w_out_all)


def _tile(s, want):
    t = min(want, s)
    assert s % t == 0 and t % CHUNK == 0
    return t


def kernel(x, c, norm_g, w_ada, b_ada, w_in, mu_shift, w0, w2, a0, a2, k_k, k_a, r_k, lnx_g, lnx_b,
           q_norm_g, k_norm_g, rel_bias, w_out):
    n_layers = w_in.shape[0]
    s = x.shape[1]
    tm = _tile(s, 512)
    ts = _tile(s, 512)
    tq = _tile(s, 256)
    mod = _ada_mod(c, w_ada, b_ada)
    bias = _bias_tables(rel_bias)
    w_in_b = w_in.astype(BF16)
    w_out_b = w_out.astype(BF16)
    zeros = jnp.zeros((LORA, D_RWKV), F32)
    for l in range(n_layers):
        rkvg, wa, att = _inproj(x, mod[l], norm_g[l], w_in_b, l, mu_shift[l], tm)
        w2a2 = jnp.concatenate([jnp.concatenate([w2[l], zeros], axis=1),
                                jnp.concatenate([zeros, a2[l]], axis=1)], axis=0).astype(BF16)
        vecs = jnp.stack([w0[l], a0[l], k_k[l], k_a[l], r_k[l].reshape(-1), lnx_g[l], lnx_b[l],
                          jnp.zeros((D_RWKV,), F32)])
        yr = _rwkv(rkvg, wa, w2a2, vecs, ts)
        ya = _attn(att, q_norm_g[l], k_norm_g[l], bias, l, tq)
        x = _outproj(x, yr, ya, mod[l], w_out_b, l, tm)
    return x
```

```python
import functools

import jax
import jax.numpy as jnp
from jax import lax
from jax.experimental import pallas as pl
from jax.experimental.pallas import tpu as pltpu

D_MODEL = 1024
CHUNK = 64
HEAD_DIM = 64
D_RWKV = 512
D_ATT = 512
N_HEADS = 8
LORA = 64
N_LEFT = 8
PAD = N_LEFT * CHUNK
BAND = PAD + CHUNK
WPAD = PAD + CHUNK
WBAND = WPAD + CHUNK
REL_CLIP = 128
N_REL = CHUNK + REL_CLIP
RMS_EPS = 1e-6
GN_EPS = 64e-5
D_SHIFT = 4 * D_RWKV + 2 * LORA
D_IN = D_SHIFT + 4 * D_ATT
PAIR = 2 * HEAD_DIM
N_PAIRS = N_HEADS // 2
QUAD = PAIR
N_QUADS = D_RWKV // QUAD
RWKV_CHUNKS_PER_STEP = 4
MIXER_STEPS_PER_TURN = (1, 1)
INPROJ_ROWS = 512
MIXER_ROWS = 512
OUTPROJ_ROWS = 1024
SUBLANES = 8
LOG2E = 1.4426950408889634
EXP_NEG_HALF = 0.6065306597126334
MASK_VALUE = -1e30

VMEM_LIMIT_V7X = 56 * 1024 * 1024

F32 = jnp.float32
BF16 = jnp.bfloat16


def _dot(a, b):
    return jnp.dot(a, b, preferred_element_type=F32)


def _dot_nt(a, b):
    return lax.dot_general(a, b, (((1,), (1,)), ((), ())), preferred_element_type=F32)


def _sigmoid(x):
    return 0.5 * jnp.tanh(0.5 * x) + 0.5


def _split2(x):
    hi = x.astype(BF16)
    lo = (x - hi.astype(F32)).astype(BF16)
    return hi, lo


def _split3(x):
    hi = x.astype(BF16)
    r1 = x - hi.astype(F32)
    mid = r1.astype(BF16)
    lo = (r1 - mid.astype(F32)).astype(BF16)
    return hi, mid, lo


def _params(sem):
    return pltpu.CompilerParams(dimension_semantics=sem, vmem_limit_bytes=VMEM_LIMIT_V7X)


def _ada_kernel(c_ref, w_ref, b_ref, o_ref):
    c = c_ref[...]
    ca = c / (1.0 + jnp.exp(-c))
    o_ref[0] = _dot(ca.astype(BF16), w_ref[0].astype(BF16)) + b_ref[0]


def _ada_mod(c, w_ada, b_ada):
    n_layers, d, d3 = w_ada.shape
    b = c.shape[0]
    tn = 1024
    return pl.pallas_call(
        _ada_kernel,
        grid=(n_layers, d3 // tn),
        in_specs=[
            pl.BlockSpec((b, d), lambda l, j: (0, 0)),
            pl.BlockSpec((1, d, tn), lambda l, j: (l, 0, j)),
            pl.BlockSpec((1, 1, tn), lambda l, j: (l, 0, j)),
        ],
        out_specs=pl.BlockSpec((1, b, tn), lambda l, j: (l, 0, j)),
        out_shape=jax.ShapeDtypeStruct((n_layers, b, d3), F32),
        compiler_params=_params(("arbitrary", "arbitrary")),
        name="ada_mod",
    )(c, w_ada, b_ada.reshape(n_layers, 1, d3))


def _bias_kernel(tbl_ref, o_ref, tab_s):
    mi = lax.broadcasted_iota(jnp.int32, (N_REL, WBAND), 0)
    ji = lax.broadcasted_iota(jnp.int32, (N_REL, WBAND), 1)
    select = (jnp.clip(WPAD - ji, -(CHUNK - 1), REL_CLIP) + (CHUNK - 1) == mi).astype(BF16)
    hi, mid, lo = _split3(tbl_ref[0] * LOG2E)
    row0 = _dot(hi, select) + _dot(mid, select) + _dot(lo, select)
    masked = ji[0:N_HEADS] < WPAD - PAD
    for q in range(CHUNK):
        rolled = row0 if q == 0 else pltpu.roll(row0, q, 1)
        tab_s[q] = jnp.where(masked, MASK_VALUE, rolled)
    for h in range(N_HEADS):
        o_ref[0, h] = tab_s[:, h, :]


def _bias_tables(rel_bias):
    n_layers = rel_bias.shape[0]
    return pl.pallas_call(
        _bias_kernel,
        grid=(n_layers,),
        in_specs=[pl.BlockSpec((1, N_HEADS, N_REL), lambda l: (l, 0, 0))],
        out_specs=pl.BlockSpec((1, N_HEADS, CHUNK, WBAND), lambda l: (l, 0, 0, 0)),
        out_shape=jax.ShapeDtypeStruct((n_layers, N_HEADS, CHUNK, WBAND), F32),
        scratch_shapes=[pltpu.VMEM((CHUNK, N_HEADS, WBAND), F32)],
        compiler_params=_params(("arbitrary",)),
        name="bias_tables",
    )(rel_bias)


def _inproj_kernel(x_ref, mod_ref, g_ref, w_ref, mu_ref, rkvg_ref, wa_ref, att_ref, carry_ref):
    i = pl.program_id(1)
    tm = x_ref.shape[1]

    @pl.when(i == 0)
    def _():
        carry_ref[...] = jnp.zeros_like(carry_ref)

    x = x_ref[0]
    y = x * lax.rsqrt(jnp.mean(x * x, axis=-1, keepdims=True) + RMS_EPS)
    shift = mod_ref[0, :, 0:D_MODEL]
    scale = mod_ref[0, :, D_MODEL:2 * D_MODEL]
    h = ((y * g_ref[...]) * (1.0 + scale) + shift).astype(BF16)

    p = _dot(h, w_ref[:, 0:D_SHIFT])
    row = lax.broadcasted_iota(jnp.int32, (tm, D_SHIFT), 0)
    prev = jnp.where(row == 0, carry_ref[...], pltpu.roll(p, 1, 0))
    carry_ref[...] = p[tm - 1:tm, :]
    ps = p + mu_ref[...] * (prev - p)
    rkvg_ref[0] = ps[:, 0:4 * D_RWKV].astype(BF16)
    wa_ref[0] = ps[:, 4 * D_RWKV:D_SHIFT].astype(BF16)
    att_ref[0] = _dot(h, w_ref[:, D_SHIFT:D_IN]).astype(BF16)


def _inproj(x, mod_l, norm_g_l, w_in_all, layer, mu_l, tm):
    b, s, d = x.shape
    return pl.pallas_call(
        _inproj_kernel,
        grid=(b, s // tm),
        in_specs=[
            pl.BlockSpec((1, tm, d), lambda bi, i: (bi, i, 0)),
            pl.BlockSpec((1, 1, 3 * d), lambda bi, i: (bi, 0, 0)),
            pl.BlockSpec((1, d), lambda bi, i: (0, 0)),
            pl.BlockSpec((None, d, D_IN), lambda bi, i: (layer, 0, 0)),
            pl.BlockSpec((1, D_SHIFT), lambda bi, i: (0, 0)),
        ],
        out_specs=[
            pl.BlockSpec((1, tm, 4 * D_RWKV), lambda bi, i: (bi, i, 0)),
            pl.BlockSpec((1, tm, 2 * LORA), lambda bi, i: (bi, i, 0)),
            pl.BlockSpec((1, tm, 4 * D_ATT), lambda bi, i: (bi, i, 0)),
        ],
        out_shape=[
            jax.ShapeDtypeStruct((b, s, 4 * D_RWKV), BF16),
            jax.ShapeDtypeStruct((b, s, 2 * LORA), BF16),
            jax.ShapeDtypeStruct((b, s, 4 * D_ATT), BF16),
        ],
        scratch_shapes=[pltpu.VMEM((1, D_SHIFT), F32)],
        compiler_params=_params(("arbitrary", "arbitrary")),
        name="inproj",
    )(x, mod_l.reshape(b, 1, 3 * d), norm_g_l.reshape(1, d), w_in_all, mu_l.reshape(1, D_SHIFT))


def _stack2(x, lane_lo):
    zero = jnp.zeros_like(x)
    return jnp.concatenate([jnp.where(lane_lo, x, zero), jnp.where(lane_lo, zero, x)], axis=0)


def _rwkv_steps(rkvg_ref, wa_ref, w2a2_ref, vec_ref, ones_ref, tri_ref, o_ref,
                st_ref, lw_s, al_s, be_s, km_s, y_s):
    i = pl.program_id(1)
    ts = rkvg_ref.shape[1]
    n_chunks = ts // CHUNK

    @pl.when(i == 0)
    def _():
        st_ref[...] = jnp.zeros_like(st_ref)

    w0 = vec_ref[0:1, :]
    a0 = vec_ref[1:2, :]
    k_k = vec_ref[2:3, :]
    k_a = vec_ref[3:4, :]
    r_k = vec_ref[4:5, :]
    lnx_g = vec_ref[5:6, :]
    lnx_b = vec_ref[6:7, :]
    ones = ones_ref[...]

    def group_sum(x, exact=False):
        hi = x.astype(BF16)
        lo = (x - hi.astype(F32)).astype(BF16) if exact else None
        parts = []
        for p in range(N_PAIRS):
            sl = slice(p * PAIR, (p + 1) * PAIR)
            acc = _dot(hi[:, sl], ones)
            parts.append(acc + _dot(lo[:, sl], ones) if exact else acc)
        return jnp.concatenate(parts, axis=1)

    def prologue(rows):
        k = rkvg_ref[0, rows, D_RWKV:2 * D_RWKV].astype(F32)
        wa = wa_ref[0, rows, :].astype(F32)
        lane_wa = lax.broadcasted_iota(jnp.int32, wa.shape, 1)
        tw = jnp.where(lane_wa < LORA, jnp.tanh(wa), wa).astype(BF16)
        za = _dot(tw, w2a2_ref[...])
        z = w0 + za[:, 0:D_RWKV]
        aa = a0 + za[:, D_RWKV:2 * D_RWKV]
        lw_s[rows, :] = -EXP_NEG_HALF * _sigmoid(z)
        a = _sigmoid(aa)
        kk = k * k_k
        kk = kk * lax.rsqrt(jnp.maximum(group_sum(kk * kk, exact=True), 1e-24))
        al_s[rows, :] = -kk
        be_s[rows, :] = kk * a
        km_s[rows, :] = k * (1.0 + (a - 1.0) * k_a)

    tri = tri_ref[...]

    cat0 = functools.partial(jnp.concatenate, axis=0)
    cat1 = functools.partial(jnp.concatenate, axis=1)
    group = RWKV_CHUNKS_PER_STEP
    rowq = lax.broadcasted_iota(jnp.int32, (CHUNK, QUAD), 0)
    laneq = lax.broadcasted_iota(jnp.int32, (CHUNK, QUAD), 1)
    colq = laneq & (HEAD_DIM - 1)
    mq_strict = colq < rowq
    mq_incl = colq <= rowq
    eyeq = (colq == rowq).astype(F32)
    head_masks = [(laneq >= e * HEAD_DIM) & (laneq < (e + 1) * HEAD_DIM) for e in range(QUAD // HEAD_DIM)]

    def s4(x):
        zero = jnp.zeros_like(x)
        return cat0([jnp.where(m, x, zero) for m in head_masks])

    assert QUAD == PAIR and CHUNK == HEAD_DIM
    pair_lo = head_masks[0]
    pair_eye = colq == rowq
    pair_bd = s4

    def chains_part(gi):
        prep = []
        for ci in range(group):
            r0 = (gi * group + ci) * CHUNK
            rows = slice(r0, r0 + CHUNK)
            r = rkvg_ref[0, rows, 0:D_RWKV].astype(F32)
            v = rkvg_ref[0, rows, 2 * D_RWKV:3 * D_RWKV]
            lw = lw_s[rows, :]
            hi, lo = _split2(lw)
            cum = _dot(tri, hi) + _dot(tri, lo)
            w_inv = jnp.exp(-cum)
            w_all = jnp.exp(jnp.sum(lw, axis=0, keepdims=True))
            bt = be_s[rows, :] * w_inv
            kt = km_s[rows, :] * w_inv
            prep.append(dict(
                rows=rows, v=v,
                rt=r * jnp.exp(cum),
                at=(al_s[rows, :] * jnp.exp(cum - lw)).astype(BF16),
                bt=bt.astype(BF16),
                kt=kt.astype(BF16),
                bw=bt * w_all,
                kw=kt * w_all,
                w_all=w_all))
        chains = [(ci, q) for ci in range(group) for q in range(N_QUADS)]
        n_ch = len(chains)
        sls = [slice(q * QUAD, (q + 1) * QUAD) for _, q in chains]
        at_c = [prep[ci]["at"][:, sl] for (ci, _), sl in zip(chains, sls)]
        rt_c = [prep[ci]["rt"][:, sl] for (ci, _), sl in zip(chains, sls)]
        v_c = [prep[ci]["v"][:, sl] for (ci, _), sl in zip(chains, sls)]
        amat = [_dot_nt(cat0([at_c[j], rt_c[j].astype(BF16)]),
                        cat0([s4(prep[ci]["bt"][:, sls[j]]), s4(prep[ci]["kt"][:, sls[j]])]))
                for j, (ci, _) in enumerate(chains)]
        yield
        n1 = [jnp.where(mq_strict, a[0:CHUNK, 0:QUAD], 0.0) for a in amat]
        a_ak = [jnp.where(mq_strict, a[0:CHUNK, QUAD:2 * QUAD], 0.0).astype(BF16) for a in amat]
        a_rb = [jnp.where(mq_incl, a[CHUNK:2 * CHUNK, 0:QUAD], 0.0).astype(BF16) for a in amat]
        a_rk = [jnp.where(mq_incl, a[CHUNK:2 * CHUNK, QUAD:2 * QUAD], 0.0).astype(BF16) for a in amat]
        n1b = [n.astype(BF16) for n in n1]
        npow = [_dot(n, s4(n)) for n in n1b]
        av = [_dot(cat0([a_ak[j], a_rk[j]]), s4(v_c[j])) for j in range(n_ch)]
        yield
        tmat = [eyeq + n for n in n1]
        for _ in range(4):
            npb = [n.astype(BF16) for n in npow]
            both = [_dot(cat0([t.astype(BF16), n]), s4(n)) for t, n in zip(tmat, npb)]
            tmat = [t + b[0:CHUNK] for t, b in zip(tmat, both)]
            npow = [b[CHUNK:2 * CHUNK] for b in both]
            yield
        tmat = [t + _dot(t.astype(BF16), s4(n.astype(BF16))) for t, n in zip(tmat, npow)]
        yield
        t2 = [_dot(tmat[j].astype(BF16), cat1([s4(at_c[j]), s4(av[j][0:CHUNK].astype(BF16))]))
              for j in range(n_ch)]
        a_pr = [t[:, 0:QUAD].astype(BF16) for t in t2]
        uv = [t[:, QUAD:2 * QUAD].astype(BF16) for t in t2]
        yield
        r2 = [_dot(a_rb[j], cat1([s4(a_pr[j]), s4(uv[j])])) for j in range(n_ch)]
        r_pr = [(rt_c[j] + r2[j][:, 0:QUAD]).astype(BF16) for j in range(n_ch)]
        y0 = [av[j][CHUNK:2 * CHUNK] + r2[j][:, QUAD:2 * QUAD] for j in range(n_ch)]
        yield
        gmat, hmat = {}, {}
        for j, (ci, q) in enumerate(chains):
            for h in range(QUAD // PAIR):
                p = q * (QUAD // PAIR) + h
                psl = slice(p * PAIR, (p + 1) * PAIR)
                hsl = slice(h * PAIR, (h + 1) * PAIR)
                v_p = prep[ci]["v"][:, psl]
                lt = cat0([prep[ci]["bw"][:, psl], prep[ci]["kw"][:, psl]]).T.astype(BF16)
                rgt = cat0([cat1([a_pr[j][:, hsl], uv[j][:, hsl]]),
                            cat1([jnp.zeros_like(v_p), v_p])])
                gh = _dot(lt, rgt)
                g_sbs = jnp.where(pair_lo, gh[0:HEAD_DIM, 0:PAIR], gh[HEAD_DIM:PAIR, 0:PAIR])
                gmat[ci, p] = (g_sbs + jnp.where(pair_eye, prep[ci]["w_all"][:, psl], 0.0)).astype(BF16)
                hmat[ci, p] = jnp.where(pair_lo, gh[0:HEAD_DIM, PAIR:2 * PAIR], gh[HEAD_DIM:PAIR, PAIR:2 * PAIR])
        yield
        for ci in range(group):
            ys = []
            for p in range(N_PAIRS):
                j = ci * N_QUADS + p // (QUAD // PAIR)
                hsl = slice((p % (QUAD // PAIR)) * PAIR, (p % (QUAD // PAIR) + 1) * PAIR)
                seq = _dot(cat0([r_pr[j][:, hsl], gmat[ci, p]]), pair_bd(st_ref[p].astype(BF16)))
                ys.append(seq[0:CHUNK] + y0[j][:, hsl])
                st_ref[p] = seq[CHUNK:CHUNK + HEAD_DIM] + hmat[ci, p]
            y_s[prep[ci]["rows"], :] = cat1(ys)
            yield

    def epilogue(rows):
        y = y_s[rows, :]
        r = rkvg_ref[0, rows, 0:D_RWKV].astype(F32)
        v = rkvg_ref[0, rows, 2 * D_RWKV:3 * D_RWKV].astype(F32)
        g = rkvg_ref[0, rows, 3 * D_RWKV:4 * D_RWKV].astype(F32)
        inv_n = 1.0 / HEAD_DIM
        yc = y - group_sum(y, exact=True) * inv_n
        yn = yc * lax.rsqrt(group_sum(yc * yc) * inv_n + GN_EPS)
        out = yn * lnx_g + lnx_b + group_sum(r * km_s[rows, :] * r_k) * v
        o_ref[0, rows, :] = (out * (g * _sigmoid(g))).astype(BF16)

    part_rows = group * CHUNK
    n_parts = ts // part_rows
    parts = [slice(k * part_rows, (k + 1) * part_rows) for k in range(n_parts)]
    for k in range(n_parts + 2):
        if k < n_parts:
            prologue(parts[k])
            yield
        if 0 <= k - 1 < n_parts:
            yield from chains_part(k - 1)
        if 0 <= k - 2 < n_parts:
            epilogue(parts[k - 2])
            yield


def _rwkv_kernel(*refs):
    for _ in _rwkv_steps(*refs):
        pass


def _rwkv(rkvg, wa, w2a2, vecs, ts):
    b, s, _ = rkvg.shape
    ones = jnp.asarray(
        (jnp.arange(PAIR)[:, None] // HEAD_DIM) == (jnp.arange(PAIR)[None, :] // HEAD_DIM), BF16)
    tri = jnp.asarray(jnp.arange(CHUNK)[:, None] >= jnp.arange(CHUNK)[None, :], BF16)
    return pl.pallas_call(
        _rwkv_kernel,
        grid=(b, s // ts),
        in_specs=[
            pl.BlockSpec((1, ts, 4 * D_RWKV), lambda bi, i: (bi, i, 0)),
            pl.BlockSpec((1, ts, 2 * LORA), lambda bi, i: (bi, i, 0)),
            pl.BlockSpec((2 * LORA, 2 * D_RWKV), lambda bi, i: (0, 0)),
            pl.BlockSpec((8, D_RWKV), lambda bi, i: (0, 0)),
            pl.BlockSpec((PAIR, PAIR), lambda bi, i: (0, 0)),
            pl.BlockSpec((CHUNK, CHUNK), lambda bi, i: (0, 0)),
        ],
        out_specs=pl.BlockSpec((1, ts, D_RWKV), lambda bi, i: (bi, i, 0)),
        out_shape=jax.ShapeDtypeStruct((b, s, D_RWKV), BF16),
        scratch_shapes=[
            pltpu.VMEM((N_PAIRS, HEAD_DIM, PAIR), F32),
            pltpu.VMEM((ts, D_RWKV), F32),
            pltpu.VMEM((ts, D_RWKV), F32),
            pltpu.VMEM((ts, D_RWKV), F32),
            pltpu.VMEM((ts, D_RWKV), F32),
            pltpu.VMEM((ts, D_RWKV), F32),
        ],
        compiler_params=_params(("arbitrary", "arbitrary")),
        name="rwkv7",
    )(rkvg, wa, w2a2, vecs, ones, tri)


def _attn_steps(q_ref, k_ref, v_ref, g_ref, qg_ref, kg_ref, ones_ref, bias_ref, o_ref,
                kp_s, vp_s, qn_s):
    i = pl.program_id(1)
    tq = q_ref.shape[1]
    s = k_ref.shape[1]
    n_chunks = tq // CHUNK
    ones = ones_ref[...]

    def head_norm(x, gain):
        hi, lo = _split2(x * x)
        parts = []
        for p in range(N_PAIRS):
            sl = slice(p * PAIR, (p + 1) * PAIR)
            parts.append(_dot(hi[:, sl], ones) + _dot(lo[:, sl], ones))
        ms = jnp.concatenate(parts, axis=1) * (1.0 / HEAD_DIM)
        return (x * lax.rsqrt(ms + RMS_EPS) * gain).astype(BF16)

    @pl.when(i == 0)
    def _():
        kp_s[0:WPAD, :] = jnp.zeros((WPAD, D_ATT), BF16)
        vp_s[0:WPAD, :] = jnp.zeros((WPAD, D_ATT), BF16)
        kp_s[WPAD:WPAD + s, :] = head_norm(k_ref[0].astype(F32), kg_ref[...])
        vp_s[WPAD:WPAD + s, :] = v_ref[0]

    lane = lax.broadcasted_iota(jnp.int32, (CHUNK, PAIR), 1)
    lane_lo = lane < HEAD_DIM
    jpos = lax.broadcasted_iota(jnp.int32, (CHUNK, WBAND), 1)
    qn_s[...] = head_norm(q_ref[0].astype(F32), qg_ref[...] * (HEAD_DIM ** -0.5 * LOG2E))
    pair_sl = [slice(p * PAIR, (p + 1) * PAIR) for p in range(N_PAIRS)]

    def scores(ci):
        win = pl.ds(pl.multiple_of(i * tq + ci * CHUNK, CHUNK), WBAND)
        return [_dot_nt(_stack2(qn_s[ci * CHUNK:(ci + 1) * CHUNK, sl], lane_lo), kp_s[win, sl])
                for sl in pair_sl]

    def finish(ci, sc):
        rows = slice(ci * CHUNK, (ci + 1) * CHUNK)
        q0 = i * tq + ci * CHUNK
        win = pl.ds(pl.multiple_of(q0, CHUNK), WBAND)
        valid = jpos >= WPAD - q0
        probs, invs = [], []
        for p in range(N_PAIRS):
            pes = []
            for e in range(2):
                se = sc[p][e * CHUNK:(e + 1) * CHUNK] + bias_ref[0, 2 * p + e]
                se = jnp.where(valid, se, MASK_VALUE)
                pe = jnp.exp2(se - jnp.max(se, axis=-1, keepdims=True))
                invs.append(1.0 / jnp.sum(pe, axis=-1, keepdims=True))
                pes.append(pe.astype(BF16))
            probs.append(jnp.concatenate(pes, axis=0))
        yield
        pv = [_dot(probs[p], vp_s[win, pair_sl[p]]) for p in range(N_PAIRS)]
        o = jnp.concatenate(
            [jnp.where(lane_lo, pv[p][0:CHUNK] * invs[2 * p], pv[p][CHUNK:2 * CHUNK] * invs[2 * p + 1])
             for p in range(N_PAIRS)], axis=1)
        g = g_ref[0, rows, :].astype(F32)
        o_ref[0, rows, :] = (o * (g * _sigmoid(g))).astype(BF16)
        yield

    sc_next = scores(0)
    yield
    for ci in range(n_chunks):
        sc = sc_next
        if ci + 1 < n_chunks:
            sc_next = scores(ci + 1)
            yield
        yield from finish(ci, sc)


def _attn_kernel(*refs):
    for _ in _attn_steps(*refs):
        pass


def _attn(att, q_g, k_g, bias_all, layer, tq):
    b, s, _ = att.shape
    ones = jnp.asarray(
        (jnp.arange(PAIR)[:, None] // HEAD_DIM) == (jnp.arange(PAIR)[None, :] // HEAD_DIM), BF16)
    qg = jnp.tile(q_g, N_HEADS).reshape(1, D_ATT)
    kg = jnp.tile(k_g, N_HEADS).reshape(1, D_ATT)
    return pl.pallas_call(
        _attn_kernel,
        grid=(b, s // tq),
        in_specs=[
            pl.BlockSpec((1, tq, D_ATT), lambda bi, i: (bi, i, 0)),
            pl.BlockSpec((1, s, D_ATT), lambda bi, i: (bi, 0, 1)),
            pl.BlockSpec((1, s, D_ATT), lambda bi, i: (bi, 0, 2)),
            pl.BlockSpec((1, tq, D_ATT), lambda bi, i: (bi, i, 3)),
            pl.BlockSpec((1, D_ATT), lambda bi, i: (0, 0)),
            pl.BlockSpec((1, D_ATT), lambda bi, i: (0, 0)),
            pl.BlockSpec((PAIR, PAIR), lambda bi, i: (0, 0)),
            pl.BlockSpec((1, N_HEADS, CHUNK, WBAND), lambda bi, i: (layer, 0, 0, 0)),
        ],
        out_specs=pl.BlockSpec((1, tq, D_ATT), lambda bi, i: (bi, i, 0)),
        out_shape=jax.ShapeDtypeStruct((b, s, D_ATT), BF16),
        scratch_shapes=[
            pltpu.VMEM((WPAD + s, D_ATT), BF16),
            pltpu.VMEM((WPAD + s, D_ATT), BF16),
            pltpu.VMEM((tq, D_ATT), BF16),
        ],
        compiler_params=_params(("arbitrary", "arbitrary")),
        name="band_attn",
    )(att, att, att, att, qg, kg, ones, bias_all)


def _mixer_kernel(rkvg_ref, wa_ref, w2a2_ref, vec_ref, ones_ref, tri_ref,
                  q_ref, k_ref, v_ref, g_ref, qg_ref, kg_ref, bias_ref,
                  yr_ref, ya_ref,
                  st_ref, lw_s, al_s, be_s, km_s, y_s, kp_s, vp_s, qn_s):
    streams = [
        _rwkv_steps(rkvg_ref, wa_ref, w2a2_ref, vec_ref, ones_ref, tri_ref, yr_ref,
                    st_ref, lw_s, al_s, be_s, km_s, y_s),
        _attn_steps(q_ref, k_ref, v_ref, g_ref, qg_ref, kg_ref, ones_ref, bias_ref, ya_ref,
                    kp_s, vp_s, qn_s),
    ]
    while streams:
        for stream, steps in list(zip(streams, MIXER_STEPS_PER_TURN)):
            for _ in range(steps):
                if stream in streams and next(stream, StopIteration) is StopIteration:
                    streams.remove(stream)


def _mixers(rkvg, wa, att, w2a2, vecs, q_g, k_g, bias_all, layer, ts):
    b, s, _ = rkvg.shape
    ones = jnp.asarray(
        (jnp.arange(PAIR)[:, None] // HEAD_DIM) == (jnp.arange(PAIR)[None, :] // HEAD_DIM), BF16)
    tri = jnp.asarray(jnp.arange(CHUNK)[:, None] >= jnp.arange(CHUNK)[None, :], BF16)
    qg = jnp.tile(q_g, N_HEADS).reshape(1, D_ATT)
    kg = jnp.tile(k_g, N_HEADS).reshape(1, D_ATT)
    const = lambda bi, i: (0, 0)
    return pl.pallas_call(
        _mixer_kernel,
        grid=(b, s // ts),
        in_specs=[
            pl.BlockSpec((1, ts, 4 * D_RWKV), lambda bi, i: (bi, i, 0)),
            pl.BlockSpec((1, ts, 2 * LORA), lambda bi, i: (bi, i, 0)),
            pl.BlockSpec((2 * LORA, 2 * D_RWKV), const),
            pl.BlockSpec((8, D_RWKV), const),
            pl.BlockSpec((PAIR, PAIR), const),
            pl.BlockSpec((CHUNK, CHUNK), const),
            pl.BlockSpec((1, ts, D_ATT), lambda bi, i: (bi, i, 0)),
            pl.BlockSpec((1, s, D_ATT), lambda bi, i: (bi, 0, 1)),
            pl.BlockSpec((1, s, D_ATT), lambda bi, i: (bi, 0, 2)),
            pl.BlockSpec((1, ts, D_ATT), lambda bi, i: (bi, i, 3)),
            pl.BlockSpec((1, D_ATT), const),
            pl.BlockSpec((1, D_ATT), const),
            pl.BlockSpec((1, N_HEADS, CHUNK, WBAND), lambda bi, i: (layer, 0, 0, 0)),
        ],
        out_specs=[
            pl.BlockSpec((1, ts, D_RWKV), lambda bi, i: (bi, i, 0)),
            pl.BlockSpec((1, ts, D_ATT), lambda bi, i: (bi, i, 0)),
        ],
        out_shape=[
            jax.ShapeDtypeStruct((b, s, D_RWKV), BF16),
            jax.ShapeDtypeStruct((b, s, D_ATT), BF16),
        ],
        scratch_shapes=[
            pltpu.VMEM((N_PAIRS, HEAD_DIM, PAIR), F32),
            pltpu.VMEM((ts, D_RWKV), F32),
            pltpu.VMEM((ts, D_RWKV), F32),
            pltpu.VMEM((ts, D_RWKV), F32),
            pltpu.VMEM((ts, D_RWKV), F32),
            pltpu.VMEM((ts, D_RWKV), F32),
            pltpu.VMEM((WPAD + s, D_ATT), BF16),
            pltpu.VMEM((WPAD + s, D_ATT), BF16),
            pltpu.VMEM((ts, D_ATT), BF16),
        ],
        compiler_params=_params(("arbitrary", "arbitrary")),
        name="mixers",
    )(rkvg, wa, w2a2, vecs, ones, tri, att, att, att, att, qg, kg, bias_all)


def _outproj_kernel(x_ref, yr_ref, ya_ref, mod_ref, w_ref, o_ref):
    acc = (_dot(yr_ref[0], w_ref[0:D_RWKV, :].astype(BF16))
           + _dot(ya_ref[0], w_ref[D_RWKV:D_RWKV + D_ATT, :].astype(BF16)))
    gate = mod_ref[0, :, 2 * D_MODEL:3 * D_MODEL]
    o_ref[0] = x_ref[0] + gate * acc


def _outproj(x, yr, ya, mod_l, w_out_all, layer, tm):
    b, s, d = x.shape
    return pl.pallas_call(
        _outproj_kernel,
        grid=(b, s // tm),
        in_specs=[
            pl.BlockSpec((1, tm, d), lambda bi, i: (bi, i, 0)),
            pl.BlockSpec((1, tm, D_RWKV), lambda bi, i: (bi, i, 0)),
            pl.BlockSpec((1, tm, D_ATT), lambda bi, i: (bi, i, 0)),
            pl.BlockSpec((1, 1, 3 * d), lambda bi, i: (bi, 0, 0)),
            pl.BlockSpec((None, D_RWKV + D_ATT, d), lambda bi, i: (layer, 0, 0)),
        ],
        out_specs=pl.BlockSpec((1, tm, d), lambda bi, i: (bi, i, 0)),
        out_shape=jax.ShapeDtypeStruct((b, s, d), F32),
        compiler_params=_params(("arbitrary", "arbitrary")),
        name="outproj",
    )(x, yr, ya, mod_l.reshape(b, 1, 3 * d), w_out_all)


def _tile(s, want):
    t = min(want, s)
    assert s % t == 0 and t % CHUNK == 0
    return t


def kernel(x, c, norm_g, w_ada, b_ada, w_in, mu_shift, w0, w2, a0, a2, k_k, k_a, r_k, lnx_g, lnx_b,
           q_norm_g, k_norm_g, rel_bias, w_out):
    n_layers = w_in.shape[0]
    s = x.shape[1]
    tm = _tile(s, INPROJ_ROWS)
    ts = _tile(s, MIXER_ROWS)
    tm_out = _tile(s, OUTPROJ_ROWS)
    mod = _ada_mod(c, w_ada, b_ada)
    bias = _bias_tables(rel_bias)
    w_in_b = w_in.astype(BF16)
    zeros = jnp.zeros((LORA, D_RWKV), F32)
    for l in range(n_layers):
        rkvg, wa, att = _inproj(x, mod[l], norm_g[l], w_in_b, l, mu_shift[l], tm)
        w2a2 = jnp.concatenate([jnp.concatenate([w2[l], zeros], axis=1),
                                jnp.concatenate([zeros, a2[l]], axis=1)], axis=0).astype(BF16)
        vecs = jnp.stack([w0[l], a0[l], k_k[l], k_a[l], r_k[l].reshape(-1), lnx_g[l], lnx_b[l],
                          jnp.zeros((D_RWKV,), F32)])
        yr, ya = _mixers(rkvg, wa, att, w2a2, vecs, q_norm_g[l], k_norm_g[l], bias, l, ts)
        x = _outproj(x, yr, ya, mod[l], w_out, l, tm_out)
    return x
```

```python
import functools

import jax
import jax.numpy as jnp
from jax import lax
from jax.experimental import pallas as pl
from jax.experimental.pallas import tpu as pltpu

D_MODEL = 1024
CHUNK = 64
HEAD_DIM = 64
D_RWKV = 512
D_ATT = 512
N_HEADS = 8
LORA = 64
N_LEFT = 8
PAD = N_LEFT * CHUNK
BAND = PAD + CHUNK
WPAD = PAD + CHUNK
WBAND = WPAD + CHUNK
REL_CLIP = 128
N_REL = CHUNK + REL_CLIP
RMS_EPS = 1e-6
GN_EPS = 64e-5
D_SHIFT = 4 * D_RWKV + 2 * LORA
D_IN = D_SHIFT + 4 * D_ATT
PAIR = 2 * HEAD_DIM
N_PAIRS = N_HEADS // 2
QUAD = PAIR
N_QUADS = D_RWKV // QUAD
RWKV_CHUNKS_PER_STEP = 4
MIXER_STEPS_PER_TURN = (1, 1)
INPROJ_ROWS = 512
MIXER_ROWS = 512
OUTPROJ_ROWS = 1024
SUBLANES = 8
LOG2E = 1.4426950408889634
EXP_NEG_HALF = 0.6065306597126334
MASK_VALUE = -1e30

VMEM_LIMIT_V7X = 56 * 1024 * 1024

F32 = jnp.float32
BF16 = jnp.bfloat16


def _dot(a, b):
    return jnp.dot(a, b, preferred_element_type=F32)


def _dot_nt(a, b):
    return lax.dot_general(a, b, (((1,), (1,)), ((), ())), preferred_element_type=F32)


def _sigmoid(x):
    return 0.5 * jnp.tanh(0.5 * x) + 0.5


def _split2(x):
    hi = x.astype(BF16)
    lo = (x - hi.astype(F32)).astype(BF16)
    return hi, lo


def _split3(x):
    hi = x.astype(BF16)
    r1 = x - hi.astype(F32)
    mid = r1.astype(BF16)
    lo = (r1 - mid.astype(F32)).astype(BF16)
    return hi, mid, lo


def _params(sem):
    return pltpu.CompilerParams(dimension_semantics=sem, vmem_limit_bytes=VMEM_LIMIT_V7X)


def _ada_kernel(c_ref, w_ref, b_ref, o_ref):
    c = c_ref[...]
    ca = c / (1.0 + jnp.exp(-c))
    o_ref[0] = _dot(ca.astype(BF16), w_ref[0].astype(BF16)) + b_ref[0]


def _ada_mod(c, w_ada, b_ada):
    n_layers, d, d3 = w_ada.shape
    b = c.shape[0]
    tn = 1024
    return pl.pallas_call(
        _ada_kernel,
        grid=(n_layers, d3 // tn),
        in_specs=[
            pl.BlockSpec((b, d), lambda l, j: (0, 0)),
            pl.BlockSpec((1, d, tn), lambda l, j: (l, 0, j)),
            pl.BlockSpec((1, 1, tn), lambda l, j: (l, 0, j)),
        ],
        out_specs=pl.BlockSpec((1, b, tn), lambda l, j: (l, 0, j)),
        out_shape=jax.ShapeDtypeStruct((n_layers, b, d3), F32),
        compiler_params=_params(("arbitrary", "arbitrary")),
        name="ada_mod",
    )(c, w_ada, b_ada.reshape(n_layers, 1, d3))


def _bias_kernel(tbl_ref, o_ref, tab_s):
    mi = lax.broadcasted_iota(jnp.int32, (N_REL, WBAND), 0)
    ji = lax.broadcasted_iota(jnp.int32, (N_REL, WBAND), 1)
    select = (jnp.clip(WPAD - ji, -(CHUNK - 1), REL_CLIP) + (CHUNK - 1) == mi).astype(BF16)
    hi, mid, lo = _split3(tbl_ref[0] * LOG2E)
    row0 = _dot(hi, select) + _dot(mid, select) + _dot(lo, select)
    masked = ji[0:N_HEADS] < WPAD - PAD
    for q in range(CHUNK):
        rolled = row0 if q == 0 else pltpu.roll(row0, q, 1)
        tab_s[q] = jnp.where(masked, MASK_VALUE, rolled)
    for h in range(N_HEADS):
        o_ref[0, h] = tab_s[:, h, :]


def _bias_tables(rel_bias):
    n_layers = rel_bias.shape[0]
    return pl.pallas_call(
        _bias_kernel,
        grid=(n_layers,),
        in_specs=[pl.BlockSpec((1, N_HEADS, N_REL), lambda l: (l, 0, 0))],
        out_specs=pl.BlockSpec((1, N_HEADS, CHUNK, WBAND), lambda l: (l, 0, 0, 0)),
        out_shape=jax.ShapeDtypeStruct((n_layers, N_HEADS, CHUNK, WBAND), F32),
        scratch_shapes=[pltpu.VMEM((CHUNK, N_HEADS, WBAND), F32)],
        compiler_params=_params(("arbitrary",)),
        name="bias_tables",
    )(rel_bias)


def _gated_residual(x, yr, ya, gate, w_out_ref):
    acc = _dot(yr, w_out_ref[0:D_RWKV, :]) + _dot(ya, w_out_ref[D_RWKV:D_RWKV + D_ATT, :])
    return x + gate * acc


def _inproj_kernel(*refs, fuse_prev):
    if fuse_prev:
        (x_ref, yr_ref, ya_ref, modp_ref, wo_ref, mod_ref, g_ref, w_ref, mu_ref,
         xo_ref, rkvg_ref, wa_ref, att_ref, carry_ref, wb_s, wob_s) = refs
    else:
        x_ref, mod_ref, g_ref, w_ref, mu_ref, rkvg_ref, wa_ref, att_ref, carry_ref, wb_s = refs
    i = pl.program_id(1)
    tm = x_ref.shape[1]

    @pl.when(jnp.logical_and(pl.program_id(0) == 0, i == 0))
    def _():
        wb_s[...] = w_ref[...].astype(BF16)
        if fuse_prev:
            wob_s[...] = wo_ref[...].astype(BF16)

    @pl.when(i == 0)
    def _():
        carry_ref[...] = jnp.zeros_like(carry_ref)

    x = x_ref[0]
    if fuse_prev:
        x = _gated_residual(x, yr_ref[0], ya_ref[0], modp_ref[0, :, 2 * D_MODEL:3 * D_MODEL], wob_s)
        xo_ref[0] = x
    y = x * lax.rsqrt(jnp.mean(x * x, axis=-1, keepdims=True) + RMS_EPS)
    shift = mod_ref[0, :, 0:D_MODEL]
    scale = mod_ref[0, :, D_MODEL:2 * D_MODEL]
    h = ((y * g_ref[...]) * (1.0 + scale) + shift).astype(BF16)

    p = _dot(h, wb_s[:, 0:D_SHIFT])
    row = lax.broadcasted_iota(jnp.int32, (tm, D_SHIFT), 0)
    prev = jnp.where(row == 0, carry_ref[...], pltpu.roll(p, 1, 0))
    carry_ref[...] = p[tm - 1:tm, :]
    ps = p + mu_ref[...] * (prev - p)
    rkvg_ref[0] = ps[:, 0:4 * D_RWKV].astype(BF16)
    wa_ref[0] = ps[:, 4 * D_RWKV:D_SHIFT].astype(BF16)
    att_ref[0] = _dot(h, wb_s[:, D_SHIFT:D_IN]).astype(BF16)


def _inproj(x, mod_l, norm_g_l, w_in_all, layer, mu_l, tm, prev=None):
    b, s, d = x.shape
    row_spec = lambda width: pl.BlockSpec((1, tm, width), lambda bi, i: (bi, i, 0))
    mod_spec = pl.BlockSpec((1, 1, 3 * d), lambda bi, i: (bi, 0, 0))
    in_specs = [
        mod_spec,
        pl.BlockSpec((1, d), lambda bi, i: (0, 0)),
        pl.BlockSpec((None, d, D_IN), lambda bi, i: (layer, 0, 0)),
        pl.BlockSpec((1, D_SHIFT), lambda bi, i: (0, 0)),
    ]
    args = [mod_l.reshape(b, 1, 3 * d), norm_g_l.reshape(1, d), w_in_all, mu_l.reshape(1, D_SHIFT)]
    out_specs = [row_spec(4 * D_RWKV), row_spec(2 * LORA), row_spec(4 * D_ATT)]
    out_shape = [
        jax.ShapeDtypeStruct((b, s, 4 * D_RWKV), BF16),
        jax.ShapeDtypeStruct((b, s, 2 * LORA), BF16),
        jax.ShapeDtypeStruct((b, s, 4 * D_ATT), BF16),
    ]
    if prev is None:
        in_specs = [row_spec(d)] + in_specs
        args = [x] + args
    else:
        yr, ya, mod_prev, w_out_all = prev
        in_specs = [
            row_spec(d), row_spec(D_RWKV), row_spec(D_ATT), mod_spec,
            pl.BlockSpec((None, D_RWKV + D_ATT, d), lambda bi, i: (layer - 1, 0, 0)),
        ] + in_specs
        args = [x, yr, ya, mod_prev.reshape(b, 1, 3 * d), w_out_all] + args
        out_specs = [row_spec(d)] + out_specs
        out_shape = [jax.ShapeDtypeStruct((b, s, d), F32)] + out_shape
    return pl.pallas_call(
        functools.partial(_inproj_kernel, fuse_prev=prev is not None),
        grid=(b, s // tm),
        in_specs=in_specs,
        out_specs=out_specs,
        out_shape=out_shape,
        scratch_shapes=[pltpu.VMEM((1, D_SHIFT), F32), pltpu.VMEM((d, D_IN), BF16)]
        + ([] if prev is None else [pltpu.VMEM((D_RWKV + D_ATT, d), BF16)]),
        compiler_params=_params(("arbitrary", "arbitrary")),
        name="inproj",
    )(*args)


def _stack2(x, lane_lo):
    zero = jnp.zeros_like(x)
    return jnp.concatenate([jnp.where(lane_lo, x, zero), jnp.where(lane_lo, zero, x)], axis=0)


def _rwkv_steps(rkvg_ref, wa_ref, w2a2_ref, vec_ref, ones_ref, tri_ref, o_ref,
                st_ref, lw_s, al_s, be_s, km_s, y_s):
    i = pl.program_id(1)
    ts = rkvg_ref.shape[1]
    n_chunks = ts // CHUNK

    @pl.when(i == 0)
    def _():
        st_ref[...] = jnp.zeros_like(st_ref)

    w0 = vec_ref[0:1, :]
    a0 = vec_ref[1:2, :]
    k_k = vec_ref[2:3, :]
    k_a = vec_ref[3:4, :]
    r_k = vec_ref[4:5, :]
    lnx_g = vec_ref[5:6, :]
    lnx_b = vec_ref[6:7, :]
    ones = ones_ref[...]

    def group_sum(x, exact=False):
        hi = x.astype(BF16)
        lo = (x - hi.astype(F32)).astype(BF16) if exact else None
        parts = []
        for p in range(N_PAIRS):
            sl = slice(p * PAIR, (p + 1) * PAIR)
            acc = _dot(hi[:, sl], ones)
            parts.append(acc + _dot(lo[:, sl], ones) if exact else acc)
        return jnp.concatenate(parts, axis=1)

    def prologue(rows):
        k = rkvg_ref[0, rows, D_RWKV:2 * D_RWKV].astype(F32)
        wa = wa_ref[0, rows, :].astype(F32)
        lane_wa = lax.broadcasted_iota(jnp.int32, wa.shape, 1)
        tw = jnp.where(lane_wa < LORA, jnp.tanh(wa), wa).astype(BF16)
        za = _dot(tw, w2a2_ref[...])
        z = w0 + za[:, 0:D_RWKV]
        aa = a0 + za[:, D_RWKV:2 * D_RWKV]
        lw_s[rows, :] = -EXP_NEG_HALF * _sigmoid(z)
        a = _sigmoid(aa)
        kk = k * k_k
        kk = kk * lax.rsqrt(jnp.maximum(group_sum(kk * kk, exact=True), 1e-24))
        al_s[rows, :] = -kk
        be_s[rows, :] = kk * a
        km_s[rows, :] = k * (1.0 + (a - 1.0) * k_a)

    tri = tri_ref[...]

    cat0 = functools.partial(jnp.concatenate, axis=0)
    cat1 = functools.partial(jnp.concatenate, axis=1)
    group = RWKV_CHUNKS_PER_STEP
    rowq = lax.broadcasted_iota(jnp.int32, (CHUNK, QUAD), 0)
    laneq = lax.broadcasted_iota(jnp.int32, (CHUNK, QUAD), 1)
    colq = laneq & (HEAD_DIM - 1)
    mq_strict = colq < rowq
    mq_incl = colq <= rowq
    eyeq = (colq == rowq).astype(F32)
    head_masks = [(laneq >= e * HEAD_DIM) & (laneq < (e + 1) * HEAD_DIM) for e in range(QUAD // HEAD_DIM)]

    def s4(x):
        zero = jnp.zeros_like(x)
        return cat0([jnp.where(m, x, zero) for m in head_masks])

    assert QUAD == PAIR and CHUNK == HEAD_DIM
    pair_lo = head_masks[0]
    pair_eye = colq == rowq
    pair_bd = s4

    def chains_part(gi):
        prep = []
        for ci in range(group):
            r0 = (gi * group + ci) * CHUNK
            rows = slice(r0, r0 + CHUNK)
            r = rkvg_ref[0, rows, 0:D_RWKV].astype(F32)
            v = rkvg_ref[0, rows, 2 * D_RWKV:3 * D_RWKV]
            lw = lw_s[rows, :]
            hi, lo = _split2(lw)
            cum = _dot(tri, hi) + _dot(tri, lo)
            w_inv = jnp.exp(-cum)
            w_all = jnp.exp(jnp.sum(lw, axis=0, keepdims=True))
            bt = be_s[rows, :] * w_inv
            kt = km_s[rows, :] * w_inv
            prep.append(dict(
                rows=rows, v=v,
                rt=r * jnp.exp(cum),
                at=(al_s[rows, :] * jnp.exp(cum - lw)).astype(BF16),
                bt=bt.astype(BF16),
                kt=kt.astype(BF16),
                bw=bt * w_all,
                kw=kt * w_all,
                w_all=w_all))
        chains = [(ci, q) for ci in range(group) for q in range(N_QUADS)]
        n_ch = len(chains)
        sls = [slice(q * QUAD, (q + 1) * QUAD) for _, q in chains]
        at_c = [prep[ci]["at"][:, sl] for (ci, _), sl in zip(chains, sls)]
        rt_c = [prep[ci]["rt"][:, sl] for (ci, _), sl in zip(chains, sls)]
        v_c = [prep[ci]["v"][:, sl] for (ci, _), sl in zip(chains, sls)]
        amat = [_dot_nt(cat0([at_c[j], rt_c[j].astype(BF16)]),
                        cat0([s4(prep[ci]["bt"][:, sls[j]]), s4(prep[ci]["kt"][:, sls[j]])]))
                for j, (ci, _) in enumerate(chains)]
        yield
        n1 = [jnp.where(mq_strict, a[0:CHUNK, 0:QUAD], 0.0) for a in amat]
        a_ak = [jnp.where(mq_strict, a[0:CHUNK, QUAD:2 * QUAD], 0.0).astype(BF16) for a in amat]
        a_rb = [jnp.where(mq_incl, a[CHUNK:2 * CHUNK, 0:QUAD], 0.0).astype(BF16) for a in amat]
        a_rk = [jnp.where(mq_incl, a[CHUNK:2 * CHUNK, QUAD:2 * QUAD], 0.0).astype(BF16) for a in amat]
        n1b = [n.astype(BF16) for n in n1]
        npow = [_dot(n, s4(n)) for n in n1b]
        av = [_dot(cat0([a_ak[j], a_rk[j]]), s4(v_c[j])) for j in range(n_ch)]
        yield
        tmat = [eyeq + n for n in n1]
        for _ in range(4):
            npb = [n.astype(BF16) for n in npow]
            both = [_dot(cat0([t.astype(BF16), n]), s4(n)) for t, n in zip(tmat, npb)]
            tmat = [t + b[0:CHUNK] for t, b in zip(tmat, both)]
            npow = [b[CHUNK:2 * CHUNK] for b in both]
            yield
        tmat = [t + _dot(t.astype(BF16), s4(n.astype(BF16))) for t, n in zip(tmat, npow)]
        yield
        t2 = [_dot(tmat[j].astype(BF16), cat1([s4(at_c[j]), s4(av[j][0:CHUNK].astype(BF16))]))
              for j in range(n_ch)]
        a_pr = [t[:, 0:QUAD].astype(BF16) for t in t2]
        uv = [t[:, QUAD:2 * QUAD].astype(BF16) for t in t2]
        yield
        r2 = [_dot(a_rb[j], cat1([s4(a_pr[j]), s4(uv[j])])) for j in range(n_ch)]
        r_pr = [(rt_c[j] + r2[j][:, 0:QUAD]).astype(BF16) for j in range(n_ch)]
        y0 = [av[j][CHUNK:2 * CHUNK] + r2[j][:, QUAD:2 * QUAD] for j in range(n_ch)]
        yield
        gmat, hmat = {}, {}
        for j, (ci, q) in enumerate(chains):
            for h in range(QUAD // PAIR):
                p = q * (QUAD // PAIR) + h
                psl = slice(p * PAIR, (p + 1) * PAIR)
                hsl = slice(h * PAIR, (h + 1) * PAIR)
                v_p = prep[ci]["v"][:, psl]
                lt = cat0([prep[ci]["bw"][:, psl], prep[ci]["kw"][:, psl]]).T.astype(BF16)
                rgt = cat0([cat1([a_pr[j][:, hsl], uv[j][:, hsl]]),
                            cat1([jnp.zeros_like(v_p), v_p])])
                gh = _dot(lt, rgt)
                g_sbs = jnp.where(pair_lo, gh[0:HEAD_DIM, 0:PAIR], gh[HEAD_DIM:PAIR, 0:PAIR])
                gmat[ci, p] = (g_sbs + jnp.where(pair_eye, prep[ci]["w_all"][:, psl], 0.0)).astype(BF16)
                hmat[ci, p] = jnp.where(pair_lo, gh[0:HEAD_DIM, PAIR:2 * PAIR], gh[HEAD_DIM:PAIR, PAIR:2 * PAIR])
        yield
        for ci in range(group):
            ys = []
            for p in range(N_PAIRS):
                j = ci * N_QUADS + p // (QUAD // PAIR)
                hsl = slice((p % (QUAD // PAIR)) * PAIR, (p % (QUAD // PAIR) + 1) * PAIR)
                seq = _dot(cat0([r_pr[j][:, hsl], gmat[ci, p]]), pair_bd(st_ref[p].astype(BF16)))
                ys.append(seq[0:CHUNK] + y0[j][:, hsl])
                st_ref[p] = seq[CHUNK:CHUNK + HEAD_DIM] + hmat[ci, p]
            y_s[prep[ci]["rows"], :] = cat1(ys)
            yield

    def epilogue(rows):
        y = y_s[rows, :]
        r = rkvg_ref[0, rows, 0:D_RWKV].astype(F32)
        v = rkvg_ref[0, rows, 2 * D_RWKV:3 * D_RWKV].astype(F32)
        g = rkvg_ref[0, rows, 3 * D_RWKV:4 * D_RWKV].astype(F32)
        inv_n = 1.0 / HEAD_DIM
        yc = y - group_sum(y, exact=True) * inv_n
        yn = yc * lax.rsqrt(group_sum(yc * yc) * inv_n + GN_EPS)
        out = yn * lnx_g + lnx_b + group_sum(r * km_s[rows, :] * r_k) * v
        o_ref[0, rows, :] = (out * (g * _sigmoid(g))).astype(BF16)

    part_rows = group * CHUNK
    n_parts = ts // part_rows
    parts = [slice(k * part_rows, (k + 1) * part_rows) for k in range(n_parts)]
    for k in range(n_parts + 2):
        if k < n_parts:
            prologue(parts[k])
            yield
        if 0 <= k - 1 < n_parts:
            yield from chains_part(k - 1)
        if 0 <= k - 2 < n_parts:
            epilogue(parts[k - 2])
            yield


def _rwkv_kernel(*refs):
    for _ in _rwkv_steps(*refs):
        pass


def _rwkv(rkvg, wa, w2a2, vecs, ts):
    b, s, _ = rkvg.shape
    ones = jnp.asarray(
        (jnp.arange(PAIR)[:, None] // HEAD_DIM) == (jnp.arange(PAIR)[None, :] // HEAD_DIM), BF16)
    tri = jnp.asarray(jnp.arange(CHUNK)[:, None] >= jnp.arange(CHUNK)[None, :], BF16)
    return pl.pallas_call(
        _rwkv_kernel,
        grid=(b, s // ts),
        in_specs=[
            pl.BlockSpec((1, ts, 4 * D_RWKV), lambda bi, i: (bi, i, 0)),
            pl.BlockSpec((1, ts, 2 * LORA), lambda bi, i: (bi, i, 0)),
            pl.BlockSpec((2 * LORA, 2 * D_RWKV), lambda bi, i: (0, 0)),
            pl.BlockSpec((8, D_RWKV), lambda bi, i: (0, 0)),
            pl.BlockSpec((PAIR, PAIR), lambda bi, i: (0, 0)),
            pl.BlockSpec((CHUNK, CHUNK), lambda bi, i: (0, 0)),
        ],
        out_specs=pl.BlockSpec((1, ts, D_RWKV), lambda bi, i: (bi, i, 0)),
        out_shape=jax.ShapeDtypeStruct((b, s, D_RWKV), BF16),
        scratch_shapes=[
            pltpu.VMEM((N_PAIRS, HEAD_DIM, PAIR), F32),
            pltpu.VMEM((ts, D_RWKV), F32),
            pltpu.VMEM((ts, D_RWKV), F32),
            pltpu.VMEM((ts, D_RWKV), F32),
            pltpu.VMEM((ts, D_RWKV), F32),
            pltpu.VMEM((ts, D_RWKV), F32),
        ],
        compiler_params=_params(("arbitrary", "arbitrary")),
        name="rwkv7",
    )(rkvg, wa, w2a2, vecs, ones, tri)


def _attn_steps(q_ref, k_ref, v_ref, g_ref, qg_ref, kg_ref, ones_ref, bias_ref, o_ref,
                kp_s, vp_s, qn_s):
    i = pl.program_id(1)
    tq = q_ref.shape[1]
    s = k_ref.shape[1]
    n_chunks = tq // CHUNK
    ones = ones_ref[...]

    def head_norm(x, gain):
        hi, lo = _split2(x * x)
        parts = []
        for p in range(N_PAIRS):
            sl = slice(p * PAIR, (p + 1) * PAIR)
            parts.append(_dot(hi[:, sl], ones) + _dot(lo[:, sl], ones))
        ms = jnp.concatenate(parts, axis=1) * (1.0 / HEAD_DIM)
        return (x * lax.rsqrt(ms + RMS_EPS) * gain).astype(BF16)

    @pl.when(i == 0)
    def _():
        kp_s[0:WPAD, :] = jnp.zeros((WPAD, D_ATT), BF16)
        vp_s[0:WPAD, :] = jnp.zeros((WPAD, D_ATT), BF16)
        kp_s[WPAD:WPAD + s, :] = head_norm(k_ref[0].astype(F32), kg_ref[...])
        vp_s[WPAD:WPAD + s, :] = v_ref[0]

    lane = lax.broadcasted_iota(jnp.int32, (CHUNK, PAIR), 1)
    lane_lo = lane < HEAD_DIM
    jpos = lax.broadcasted_iota(jnp.int32, (CHUNK, WBAND), 1)
    qn_s[...] = head_norm(q_ref[0].astype(F32), qg_ref[...] * (HEAD_DIM ** -0.5 * LOG2E))
    pair_sl = [slice(p * PAIR, (p + 1) * PAIR) for p in range(N_PAIRS)]

    def scores(ci):
        win = pl.ds(pl.multiple_of(i * tq + ci * CHUNK, CHUNK), WBAND)
        return [_dot_nt(_stack2(qn_s[ci * CHUNK:(ci + 1) * CHUNK, sl], lane_lo), kp_s[win, sl])
                for sl in pair_sl]

    def finish(ci, sc):
        rows = slice(ci * CHUNK, (ci + 1) * CHUNK)
        q0 = i * tq + ci * CHUNK
        win = pl.ds(pl.multiple_of(q0, CHUNK), WBAND)
        valid = jpos >= WPAD - q0
        probs, invs = [], []
        for p in range(N_PAIRS):
            pes = []
            for e in range(2):
                se = sc[p][e * CHUNK:(e + 1) * CHUNK] + bias_ref[0, 2 * p + e]
                se = jnp.where(valid, se, MASK_VALUE)
                pe = jnp.exp2(se - jnp.max(se, axis=-1, keepdims=True))
                invs.append(1.0 / jnp.sum(pe, axis=-1, keepdims=True))
                pes.append(pe.astype(BF16))
            probs.append(jnp.concatenate(pes, axis=0))
        yield
        pv = [_dot(probs[p], vp_s[win, pair_sl[p]]) for p in range(N_PAIRS)]
        o = jnp.concatenate(
            [jnp.where(lane_lo, pv[p][0:CHUNK] * invs[2 * p], pv[p][CHUNK:2 * CHUNK] * invs[2 * p + 1])
             for p in range(N_PAIRS)], axis=1)
        g = g_ref[0, rows, :].astype(F32)
        o_ref[0, rows, :] = (o * (g * _sigmoid(g))).astype(BF16)
        yield

    sc_next = scores(0)
    yield
    for ci in range(n_chunks):
        sc = sc_next
        if ci + 1 < n_chunks:
            sc_next = scores(ci + 1)
            yield
        yield from finish(ci, sc)


def _attn_kernel(*refs):
    for _ in _attn_steps(*refs):
        pass


def _attn(att, q_g, k_g, bias_all, layer, tq):
    b, s, _ = att.shape
    ones = jnp.asarray(
        (jnp.arange(PAIR)[:, None] // HEAD_DIM) == (jnp.arange(PAIR)[None, :] // HEAD_DIM), BF16)
    qg = jnp.tile(q_g, N_HEADS).reshape(1, D_ATT)
    kg = jnp.tile(k_g, N_HEADS).reshape(1, D_ATT)
    return pl.pallas_call(
        _attn_kernel,
        grid=(b, s // tq),
        in_specs=[
            pl.BlockSpec((1, tq, D_ATT), lambda bi, i: (bi, i, 0)),
            pl.BlockSpec((1, s, D_ATT), lambda bi, i: (bi, 0, 1)),
            pl.BlockSpec((1, s, D_ATT), lambda bi, i: (bi, 0, 2)),
            pl.BlockSpec((1, tq, D_ATT), lambda bi, i: (bi, i, 3)),
            pl.BlockSpec((1, D_ATT), lambda bi, i: (0, 0)),
            pl.BlockSpec((1, D_ATT), lambda bi, i: (0, 0)),
            pl.BlockSpec((PAIR, PAIR), lambda bi, i: (0, 0)),
            pl.BlockSpec((1, N_HEADS, CHUNK, WBAND), lambda bi, i: (layer, 0, 0, 0)),
        ],
        out_specs=pl.BlockSpec((1, tq, D_ATT), lambda bi, i: (bi, i, 0)),
        out_shape=jax.ShapeDtypeStruct((b, s, D_ATT), BF16),
        scratch_shapes=[
            pltpu.VMEM((WPAD + s, D_ATT), BF16),
            pltpu.VMEM((WPAD + s, D_ATT), BF16),
            pltpu.VMEM((tq, D_ATT), BF16),
        ],
        compiler_params=_params(("arbitrary", "arbitrary")),
        name="band_attn",
    )(att, att, att, att, qg, kg, ones, bias_all)


def _mixer_kernel(rkvg_ref, wa_ref, w2a2_ref, vec_ref, ones_ref, tri_ref,
                  q_ref, k_ref, v_ref, g_ref, qg_ref, kg_ref, bias_ref,
                  yr_ref, ya_ref,
                  st_ref, lw_s, al_s, be_s, km_s, y_s, kp_s, vp_s, qn_s):
    streams = [
        _rwkv_steps(rkvg_ref, wa_ref, w2a2_ref, vec_ref, ones_ref, tri_ref, yr_ref,
                    st_ref, lw_s, al_s, be_s, km_s, y_s),
        _attn_steps(q_ref, k_ref, v_ref, g_ref, qg_ref, kg_ref, ones_ref, bias_ref, ya_ref,
                    kp_s, vp_s, qn_s),
    ]
    while streams:
        for stream, steps in list(zip(streams, MIXER_STEPS_PER_TURN)):
            for _ in range(steps):
                if stream in streams and next(stream, StopIteration) is StopIteration:
                    streams.remove(stream)


def _mixers(rkvg, wa, att, w2a2, vecs, q_g, k_g, bias_all, layer, ts):
    b, s, _ = rkvg.shape
    ones = jnp.asarray(
        (jnp.arange(PAIR)[:, None] // HEAD_DIM) == (jnp.arange(PAIR)[None, :] // HEAD_DIM), BF16)
    tri = jnp.asarray(jnp.arange(CHUNK)[:, None] >= jnp.arange(CHUNK)[None, :], BF16)
    qg = jnp.tile(q_g, N_HEADS).reshape(1, D_ATT)
    kg = jnp.tile(k_g, N_HEADS).reshape(1, D_ATT)
    const = lambda bi, i: (0, 0)
    return pl.pallas_call(
        _mixer_kernel,
        grid=(b, s // ts),
        in_specs=[
            pl.BlockSpec((1, ts, 4 * D_RWKV), lambda bi, i: (bi, i, 0)),
            pl.BlockSpec((1, ts, 2 * LORA), lambda bi, i: (bi, i, 0)),
            pl.BlockSpec((2 * LORA, 2 * D_RWKV), const),
            pl.BlockSpec((8, D_RWKV), const),
            pl.BlockSpec((PAIR, PAIR), const),
            pl.BlockSpec((CHUNK, CHUNK), const),
            pl.BlockSpec((1, ts, D_ATT), lambda bi, i: (bi, i, 0)),
            pl.BlockSpec((1, s, D_ATT), lambda bi, i: (bi, 0, 1)),
            pl.BlockSpec((1, s, D_ATT), lambda bi, i: (bi, 0, 2)),
            pl.BlockSpec((1, ts, D_ATT), lambda bi, i: (bi, i, 3)),
            pl.BlockSpec((1, D_ATT), const),
            pl.BlockSpec((1, D_ATT), const),
            pl.BlockSpec((1, N_HEADS, CHUNK, WBAND), lambda bi, i: (layer, 0, 0, 0)),
        ],
        out_specs=[
            pl.BlockSpec((1, ts, D_RWKV), lambda bi, i: (bi, i, 0)),
            pl.BlockSpec((1, ts, D_ATT), lambda bi, i: (bi, i, 0)),
        ],
        out_shape=[
            jax.ShapeDtypeStruct((b, s, D_RWKV), BF16),
            jax.ShapeDtypeStruct((b, s, D_ATT), BF16),
        ],
        scratch_shapes=[
            pltpu.VMEM((N_PAIRS, HEAD_DIM, PAIR), F32),
            pltpu.VMEM((ts, D_RWKV), F32),
            pltpu.VMEM((ts, D_RWKV), F32),
            pltpu.VMEM((ts, D_RWKV), F32),
            pltpu.VMEM((ts, D_RWKV), F32),
            pltpu.VMEM((ts, D_RWKV), F32),
            pltpu.VMEM((WPAD + s, D_ATT), BF16),
            pltpu.VMEM((WPAD + s, D_ATT), BF16),
            pltpu.VMEM((ts, D_ATT), BF16),
        ],
        compiler_params=_params(("arbitrary", "arbitrary")),
        name="mixers",
    )(rkvg, wa, w2a2, vecs, ones, tri, att, att, att, att, qg, kg, bias_all)


def _outproj_kernel(x_ref, yr_ref, ya_ref, mod_ref, w_ref, o_ref, wb_s):
    @pl.when(jnp.logical_and(pl.program_id(0) == 0, pl.program_id(1) == 0))
    def _():
        wb_s[...] = w_ref[...].astype(BF16)

    gate = mod_ref[0, :, 2 * D_MODEL:3 * D_MODEL]
    o_ref[0] = _gated_residual(x_ref[0], yr_ref[0], ya_ref[0], gate, wb_s)


def _outproj(x, yr, ya, mod_l, w_out_all, layer, tm):
    b, s, d = x.shape
    return pl.pallas_call(
        _outproj_kernel,
        grid=(b, s // tm),
        in_specs=[
            pl.BlockSpec((1, tm, d), lambda bi, i: (bi, i, 0)),
            pl.BlockSpec((1, tm, D_RWKV), lambda bi, i: (bi, i, 0)),
            pl.BlockSpec((1, tm, D_ATT), lambda bi, i: (bi, i, 0)),
            pl.BlockSpec((1, 1, 3 * d), lambda bi, i: (bi, 0, 0)),
            pl.BlockSpec((None, D_RWKV + D_ATT, d), lambda bi, i: (layer, 0, 0)),
        ],
        out_specs=pl.BlockSpec((1, tm, d), lambda bi, i: (bi, i, 0)),
        out_shape=jax.ShapeDtypeStruct((b, s, d), F32),
        scratch_shapes=[pltpu.VMEM((D_RWKV + D_ATT, d), BF16)],
        compiler_params=_params(("arbitrary", "arbitrary")),
        name="outproj",
    )(x, yr, ya, mod_l.reshape(b, 1, 3 * d), w_out_all)


def _tile(s, want):
    t = min(want, s)
    assert s % t == 0 and t % CHUNK == 0
    return t


def kernel(x, c, norm_g, w_ada, b_ada, w_in, mu_shift, w0, w2, a0, a2, k_k, k_a, r_k, lnx_g, lnx_b,
           q_norm_g, k_norm_g, rel_bias, w_out):
    n_layers = w_in.shape[0]
    s = x.shape[1]
    tm = _tile(s, INPROJ_ROWS)
    ts = _tile(s, MIXER_ROWS)
    tm_out = _tile(s, OUTPROJ_ROWS)
    mod = _ada_mod(c, w_ada, b_ada)
    bias = _bias_tables(rel_bias)
    zeros = jnp.zeros((LORA, D_RWKV), F32)
    prev = None
    for l in range(n_layers):
        if prev is None:
            rkvg, wa, att = _inproj(x, mod[l], norm_g[l], w_in, l, mu_shift[l], tm)
        else:
            x, rkvg, wa, att = _inproj(x, mod[l], norm_g[l], w_in, l, mu_shift[l], tm, prev=prev)
        w2a2 = jnp.concatenate([jnp.concatenate([w2[l], zeros], axis=1),
                                jnp.concatenate([zeros, a2[l]], axis=1)], axis=0).astype(BF16)
        vecs = jnp.stack([w0[l], a0[l], k_k[l], k_a[l], r_k[l].reshape(-1), lnx_g[l], lnx_b[l],
                          jnp.zeros((D_RWKV,), F32)])
        yr, ya = _mixers(rkvg, wa, att, w2a2, vecs, q_norm_g[l], k_norm_g[l], bias, l, ts)
        prev = (yr, ya, mod[l], w_out)
    return _outproj(x, yr, ya, mod[n_layers - 1], w_out, n_layers - 1, tm_out)
```

```python
import functools

import jax
import jax.numpy as jnp
from jax import lax
from jax.experimental import pallas as pl
from jax.experimental.pallas import tpu as pltpu

D_MODEL = 1024
CHUNK = 64
HEAD_DIM = 64
D_RWKV = 512
D_ATT = 512
N_HEADS = 8
LORA = 64
N_LEFT = 8
PAD = N_LEFT * CHUNK
BAND = PAD + CHUNK
WPAD = PAD + CHUNK
WBAND = WPAD + CHUNK
REL_CLIP = 128
N_REL = CHUNK + REL_CLIP
RMS_EPS = 1e-6
GN_EPS = 64e-5
D_SHIFT = 4 * D_RWKV + 2 * LORA
D_IN = D_SHIFT + 4 * D_ATT
PAIR = 2 * HEAD_DIM
N_PAIRS = N_HEADS // 2
QUAD = PAIR
N_QUADS = D_RWKV // QUAD
RWKV_CHUNKS_PER_STEP = 4
MIXER_STEPS_PER_TURN = (1, 1)
INPROJ_ROWS = 512
INPROJ_PARTS = 2
MIXER_ROWS = 512
OUTPROJ_ROWS = 1024
SUBLANES = 8
LOG2E = 1.4426950408889634
EXP_NEG_HALF = 0.6065306597126334
MASK_VALUE = -1e30

VMEM_LIMIT_V7X = 56 * 1024 * 1024

F32 = jnp.float32
BF16 = jnp.bfloat16


def _dot(a, b):
    return jnp.dot(a, b, preferred_element_type=F32)


def _dot_nt(a, b):
    return lax.dot_general(a, b, (((1,), (1,)), ((), ())), preferred_element_type=F32)


def _sigmoid(x):
    return 0.5 * jnp.tanh(0.5 * x) + 0.5


def _split2(x):
    hi = x.astype(BF16)
    lo = (x - hi.astype(F32)).astype(BF16)
    return hi, lo


def _split3(x):
    hi = x.astype(BF16)
    r1 = x - hi.astype(F32)
    mid = r1.astype(BF16)
    lo = (r1 - mid.astype(F32)).astype(BF16)
    return hi, mid, lo


def _params(sem):
    return pltpu.CompilerParams(dimension_semantics=sem, vmem_limit_bytes=VMEM_LIMIT_V7X)


def _ada_kernel(c_ref, w_ref, b_ref, o_ref):
    c = c_ref[...]
    ca = c / (1.0 + jnp.exp(-c))
    o_ref[0] = _dot(ca.astype(BF16), w_ref[0].astype(BF16)) + b_ref[0]


def _ada_mod(c, w_ada, b_ada):
    n_layers, d, d3 = w_ada.shape
    b = c.shape[0]
    tn = 1024
    return pl.pallas_call(
        _ada_kernel,
        grid=(n_layers, d3 // tn),
        in_specs=[
            pl.BlockSpec((b, d), lambda l, j: (0, 0)),
            pl.BlockSpec((1, d, tn), lambda l, j: (l, 0, j)),
            pl.BlockSpec((1, 1, tn), lambda l, j: (l, 0, j)),
        ],
        out_specs=pl.BlockSpec((1, b, tn), lambda l, j: (l, 0, j)),
        out_shape=jax.ShapeDtypeStruct((n_layers, b, d3), F32),
        compiler_params=_params(("arbitrary", "arbitrary")),
        name="ada_mod",
    )(c, w_ada, b_ada.reshape(n_layers, 1, d3))


def _bias_kernel(tbl_ref, o_ref, tab_s):
    mi = lax.broadcasted_iota(jnp.int32, (N_REL, WBAND), 0)
    ji = lax.broadcasted_iota(jnp.int32, (N_REL, WBAND), 1)
    select = (jnp.clip(WPAD - ji, -(CHUNK - 1), REL_CLIP) + (CHUNK - 1) == mi).astype(BF16)
    hi, mid, lo = _split3(tbl_ref[0] * LOG2E)
    row0 = _dot(hi, select) + _dot(mid, select) + _dot(lo, select)
    masked = ji[0:N_HEADS] < WPAD - PAD
    for q in range(CHUNK):
        rolled = row0 if q == 0 else pltpu.roll(row0, q, 1)
        tab_s[q] = jnp.where(masked, MASK_VALUE, rolled)
    for h in range(N_HEADS):
        o_ref[0, h] = tab_s[:, h, :]


def _bias_tables(rel_bias):
    n_layers = rel_bias.shape[0]
    return pl.pallas_call(
        _bias_kernel,
        grid=(n_layers,),
        in_specs=[pl.BlockSpec((1, N_HEADS, N_REL), lambda l: (l, 0, 0))],
        out_specs=pl.BlockSpec((1, N_HEADS, CHUNK, WBAND), lambda l: (l, 0, 0, 0)),
        out_shape=jax.ShapeDtypeStruct((n_layers, N_HEADS, CHUNK, WBAND), F32),
        scratch_shapes=[pltpu.VMEM((CHUNK, N_HEADS, WBAND), F32)],
        compiler_params=_params(("arbitrary",)),
        name="bias_tables",
    )(rel_bias)


def _gated_residual(x, yr, ya, gate, w_out_ref):
    acc = _dot(yr, w_out_ref[0:D_RWKV, :]) + _dot(ya, w_out_ref[D_RWKV:D_RWKV + D_ATT, :])
    return x + gate * acc


def _inproj_kernel(*refs, fuse_prev):
    if fuse_prev:
        (x_ref, yr_ref, ya_ref, modp_ref, wo_ref, mod_ref, g_ref, w_ref, mu_ref,
         xo_ref, rkvg_ref, wa_ref, att_ref, carry_ref, wb_s, wob_s) = refs
    else:
        x_ref, mod_ref, g_ref, w_ref, mu_ref, rkvg_ref, wa_ref, att_ref, carry_ref, wb_s = refs
    i = pl.program_id(1)
    tm = x_ref.shape[1]

    @pl.when(jnp.logical_and(pl.program_id(0) == 0, i == 0))
    def _():
        wb_s[...] = w_ref[...].astype(BF16)
        if fuse_prev:
            wob_s[...] = wo_ref[...].astype(BF16)

    @pl.when(i == 0)
    def _():
        carry_ref[...] = jnp.zeros_like(carry_ref)

    shift = mod_ref[0, :, 0:D_MODEL]
    scale = mod_ref[0, :, D_MODEL:2 * D_MODEL]

    def front(rows):
        x = x_ref[0, rows, :]
        if fuse_prev:
            gate = modp_ref[0, :, 2 * D_MODEL:3 * D_MODEL]
            x = _gated_residual(x, yr_ref[0, rows, :], ya_ref[0, rows, :], gate, wob_s)
            xo_ref[0, rows, :] = x
        y = x * lax.rsqrt(jnp.mean(x * x, axis=-1, keepdims=True) + RMS_EPS)
        return ((y * g_ref[...]) * (1.0 + scale) + shift).astype(BF16)

    def back(rows, h, carry):
        n = rows.stop - rows.start
        p = _dot(h, wb_s[:, 0:D_SHIFT])
        row = lax.broadcasted_iota(jnp.int32, (n, D_SHIFT), 0)
        prev = jnp.where(row == 0, carry, pltpu.roll(p, 1, 0))
        ps = p + mu_ref[...] * (prev - p)
        rkvg_ref[0, rows, :] = ps[:, 0:4 * D_RWKV].astype(BF16)
        wa_ref[0, rows, :] = ps[:, 4 * D_RWKV:D_SHIFT].astype(BF16)
        att_ref[0, rows, :] = _dot(h, wb_s[:, D_SHIFT:D_IN]).astype(BF16)
        return p[n - 1:n, :]

    part = tm // INPROJ_PARTS
    parts = [slice(k * part, (k + 1) * part) for k in range(INPROJ_PARTS)]
    hs = [front(rows) for rows in parts]
    carry = carry_ref[...]
    for rows, h in zip(parts, hs):
        carry = back(rows, h, carry)
    carry_ref[...] = carry


def _inproj(x, mod_l, norm_g_l, w_in_all, layer, mu_l, tm, prev=None):
    b, s, d = x.shape
    row_spec = lambda width: pl.BlockSpec((1, tm, width), lambda bi, i: (bi, i, 0))
    mod_spec = pl.BlockSpec((1, 1, 3 * d), lambda bi, i: (bi, 0, 0))
    in_specs = [
        mod_spec,
        pl.BlockSpec((1, d), lambda bi, i: (0, 0)),
        pl.BlockSpec((None, d, D_IN), lambda bi, i: (layer, 0, 0)),
        pl.BlockSpec((1, D_SHIFT), lambda bi, i: (0, 0)),
    ]
    args = [mod_l.reshape(b, 1, 3 * d), norm_g_l.reshape(1, d), w_in_all, mu_l.reshape(1, D_SHIFT)]
    out_specs = [row_spec(4 * D_RWKV), row_spec(2 * LORA), row_spec(4 * D_ATT)]
    out_shape = [
        jax.ShapeDtypeStruct((b, s, 4 * D_RWKV), BF16),
        jax.ShapeDtypeStruct((b, s, 2 * LORA), BF16),
        jax.ShapeDtypeStruct((b, s, 4 * D_ATT), BF16),
    ]
    if prev is None:
        in_specs = [row_spec(d)] + in_specs
        args = [x] + args
    else:
        yr, ya, mod_prev, w_out_all = prev
        in_specs = [
            row_spec(d), row_spec(D_RWKV), row_spec(D_ATT), mod_spec,
            pl.BlockSpec((None, D_RWKV + D_ATT, d), lambda bi, i: (layer - 1, 0, 0)),
        ] + in_specs
        args = [x, yr, ya, mod_prev.reshape(b, 1, 3 * d), w_out_all] + args
        out_specs = [row_spec(d)] + out_specs
        out_shape = [jax.ShapeDtypeStruct((b, s, d), F32)] + out_shape
    return pl.pallas_call(
        functools.partial(_inproj_kernel, fuse_prev=prev is not None),
        grid=(b, s // tm),
        in_specs=in_specs,
        out_specs=out_specs,
        out_shape=out_shape,
        scratch_shapes=[pltpu.VMEM((1, D_SHIFT), F32), pltpu.VMEM((d, D_IN), BF16)]
        + ([] if prev is None else [pltpu.VMEM((D_RWKV + D_ATT, d), BF16)]),
        compiler_params=_params(("arbitrary", "arbitrary")),
        name="inproj",
    )(*args)


def _stack2(x, lane_lo):
    zero = jnp.zeros_like(x)
    return jnp.concatenate([jnp.where(lane_lo, x, zero), jnp.where(lane_lo, zero, x)], axis=0)


def _rwkv_steps(rkvg_ref, wa_ref, w2a2_ref, vec_ref, ones_ref, tri_ref, o_ref,
                st_ref, lw_s, al_s, be_s, km_s, y_s):
    i = pl.program_id(1)
    ts = rkvg_ref.shape[1]
    n_chunks = ts // CHUNK

    @pl.when(i == 0)
    def _():
        st_ref[...] = jnp.zeros_like(st_ref)

    w0 = vec_ref[0:1, :]
    a0 = vec_ref[1:2, :]
    k_k = vec_ref[2:3, :]
    k_a = vec_ref[3:4, :]
    r_k = vec_ref[4:5, :]
    lnx_g = vec_ref[5:6, :]
    lnx_b = vec_ref[6:7, :]
    ones = ones_ref[...]

    def group_sum(x, exact=False):
        hi = x.astype(BF16)
        lo = (x - hi.astype(F32)).astype(BF16) if exact else None
        parts = []
        for p in range(N_PAIRS):
            sl = slice(p * PAIR, (p + 1) * PAIR)
            acc = _dot(hi[:, sl], ones)
            parts.append(acc + _dot(lo[:, sl], ones) if exact else acc)
        return jnp.concatenate(parts, axis=1)

    def prologue(rows):
        k = rkvg_ref[0, rows, D_RWKV:2 * D_RWKV].astype(F32)
        wa = wa_ref[0, rows, :].astype(F32)
        lane_wa = lax.broadcasted_iota(jnp.int32, wa.shape, 1)
        tw = jnp.where(lane_wa < LORA, jnp.tanh(wa), wa).astype(BF16)
        za = _dot(tw, w2a2_ref[...])
        z = w0 + za[:, 0:D_RWKV]
        aa = a0 + za[:, D_RWKV:2 * D_RWKV]
        lw_s[rows, :] = -EXP_NEG_HALF * _sigmoid(z)
        a = _sigmoid(aa)
        kk = k * k_k
        kk = kk * lax.rsqrt(jnp.maximum(group_sum(kk * kk), 1e-24))
        al_s[rows, :] = -kk
        be_s[rows, :] = kk * a
        km_s[rows, :] = k * (1.0 + (a - 1.0) * k_a)

    tri = tri_ref[...]

    cat0 = functools.partial(jnp.concatenate, axis=0)
    cat1 = functools.partial(jnp.concatenate, axis=1)
    group = RWKV_CHUNKS_PER_STEP
    rowq = lax.broadcasted_iota(jnp.int32, (CHUNK, QUAD), 0)
    laneq = lax.broadcasted_iota(jnp.int32, (CHUNK, QUAD), 1)
    colq = laneq & (HEAD_DIM - 1)
    mq_strict = colq < rowq
    mq_incl = colq <= rowq
    eyeq = (colq == rowq).astype(F32)
    head_masks = [(laneq >= e * HEAD_DIM) & (laneq < (e + 1) * HEAD_DIM) for e in range(QUAD // HEAD_DIM)]

    def s4(x):
        zero = jnp.zeros_like(x)
        return cat0([jnp.where(m, x, zero) for m in head_masks])

    assert QUAD == PAIR and CHUNK == HEAD_DIM
    pair_lo = head_masks[0]
    pair_eye = colq == rowq
    pair_bd = s4

    def chains_part(gi):
        prep = []
        for ci in range(group):
            r0 = (gi * group + ci) * CHUNK
            rows = slice(r0, r0 + CHUNK)
            r = rkvg_ref[0, rows, 0:D_RWKV].astype(F32)
            v = rkvg_ref[0, rows, 2 * D_RWKV:3 * D_RWKV]
            lw = lw_s[rows, :]
            hi, lo = _split2(lw)
            cum = _dot(tri, hi) + _dot(tri, lo)
            w_inv = jnp.exp(-cum)
            w_all = jnp.exp(jnp.sum(lw, axis=0, keepdims=True))
            bt = be_s[rows, :] * w_inv
            kt = km_s[rows, :] * w_inv
            prep.append(dict(
                rows=rows, v=v,
                rt=r * jnp.exp(cum),
                at=(al_s[rows, :] * jnp.exp(cum - lw)).astype(BF16),
                bt=bt.astype(BF16),
                kt=kt.astype(BF16),
                bw=bt * w_all,
                kw=kt * w_all,
                w_all=w_all))
        chains = [(ci, q) for ci in range(group) for q in range(N_QUADS)]
        n_ch = len(chains)
        sls = [slice(q * QUAD, (q + 1) * QUAD) for _, q in chains]
        at_c = [prep[ci]["at"][:, sl] for (ci, _), sl in zip(chains, sls)]
        rt_c = [prep[ci]["rt"][:, sl] for (ci, _), sl in zip(chains, sls)]
        v_c = [prep[ci]["v"][:, sl] for (ci, _), sl in zip(chains, sls)]
        amat = [_dot_nt(cat0([at_c[j], rt_c[j].astype(BF16)]),
                        cat0([s4(prep[ci]["bt"][:, sls[j]]), s4(prep[ci]["kt"][:, sls[j]])]))
                for j, (ci, _) in enumerate(chains)]
        yield
        n1 = [jnp.where(mq_strict, a[0:CHUNK, 0:QUAD], 0.0) for a in amat]
        a_ak = [jnp.where(mq_strict, a[0:CHUNK, QUAD:2 * QUAD], 0.0).astype(BF16) for a in amat]
        a_rb = [jnp.where(mq_incl, a[CHUNK:2 * CHUNK, 0:QUAD], 0.0).astype(BF16) for a in amat]
        a_rk = [jnp.where(mq_incl, a[CHUNK:2 * CHUNK, QUAD:2 * QUAD], 0.0).astype(BF16) for a in amat]
        n1b = [n.astype(BF16) for n in n1]
        npow = [_dot(n, s4(n)) for n in n1b]
        av = [_dot(cat0([a_ak[j], a_rk[j]]), s4(v_c[j])) for j in range(n_ch)]
        yield
        tmat = [eyeq + n for n in n1]
        for _ in range(4):
            npb = [n.astype(BF16) for n in npow]
            both = [_dot(cat0([t.astype(BF16), n]), s4(n)) for t, n in zip(tmat, npb)]
            tmat = [t + b[0:CHUNK] for t, b in zip(tmat, both)]
            npow = [b[CHUNK:2 * CHUNK] for b in both]
            yield
        tmat = [t + _dot(t.astype(BF16), s4(n.astype(BF16))) for t, n in zip(tmat, npow)]
        yield
        t2 = [_dot(tmat[j].astype(BF16), cat1([s4(at_c[j]), s4(av[j][0:CHUNK].astype(BF16))]))
              for j in range(n_ch)]
        a_pr = [t[:, 0:QUAD].astype(BF16) for t in t2]
        uv = [t[:, QUAD:2 * QUAD].astype(BF16) for t in t2]
        yield
        r2 = [_dot(a_rb[j], cat1([s4(a_pr[j]), s4(uv[j])])) for j in range(n_ch)]
        r_pr = [(rt_c[j] + r2[j][:, 0:QUAD]).astype(BF16) for j in range(n_ch)]
        y0 = [av[j][CHUNK:2 * CHUNK] + r2[j][:, QUAD:2 * QUAD] for j in range(n_ch)]
        yield
        gmat, hmat = {}, {}
        for j, (ci, q) in enumerate(chains):
            for h in range(QUAD // PAIR):
                p = q * (QUAD // PAIR) + h
                psl = slice(p * PAIR, (p + 1) * PAIR)
                hsl = slice(h * PAIR, (h + 1) * PAIR)
                v_p = prep[ci]["v"][:, psl]
                lt = cat0([prep[ci]["bw"][:, psl], prep[ci]["kw"][:, psl]]).T.astype(BF16)
                rgt = cat0([cat1([a_pr[j][:, hsl], uv[j][:, hsl]]),
                            cat1([jnp.zeros_like(v_p), v_p])])
                gh = _dot(lt, rgt)
                g_sbs = jnp.where(pair_lo, gh[0:HEAD_DIM, 0:PAIR], gh[HEAD_DIM:PAIR, 0:PAIR])
                gmat[ci, p] = (g_sbs + jnp.where(pair_eye, prep[ci]["w_all"][:, psl], 0.0)).astype(BF16)
                hmat[ci, p] = jnp.where(pair_lo, gh[0:HEAD_DIM, PAIR:2 * PAIR], gh[HEAD_DIM:PAIR, PAIR:2 * PAIR])
        yield
        for ci in range(group):
            ys = []
            for p in range(N_PAIRS):
                j = ci * N_QUADS + p // (QUAD // PAIR)
                hsl = slice((p % (QUAD // PAIR)) * PAIR, (p % (QUAD // PAIR) + 1) * PAIR)
                seq = _dot(cat0([r_pr[j][:, hsl], gmat[ci, p]]), pair_bd(st_ref[p].astype(BF16)))
                ys.append(seq[0:CHUNK] + y0[j][:, hsl])
                st_ref[p] = seq[CHUNK:CHUNK + HEAD_DIM] + hmat[ci, p]
            y_s[prep[ci]["rows"], :] = cat1(ys)
            yield

    def epilogue(rows):
        y = y_s[rows, :]
        r = rkvg_ref[0, rows, 0:D_RWKV].astype(F32)
        v = rkvg_ref[0, rows, 2 * D_RWKV:3 * D_RWKV].astype(F32)
        g = rkvg_ref[0, rows, 3 * D_RWKV:4 * D_RWKV].astype(F32)
        inv_n = 1.0 / HEAD_DIM
        yc = y - group_sum(y, exact=True) * inv_n
        yn = yc * lax.rsqrt(group_sum(yc * yc) * inv_n + GN_EPS)
        out = yn * lnx_g + lnx_b + group_sum(r * km_s[rows, :] * r_k) * v
        o_ref[0, rows, :] = (out * (g * _sigmoid(g))).astype(BF16)

    part_rows = group * CHUNK
    n_parts = ts // part_rows
    parts = [slice(k * part_rows, (k + 1) * part_rows) for k in range(n_parts)]
    for k in range(n_parts + 2):
        if k < n_parts:
            prologue(parts[k])
            yield
        if 0 <= k - 1 < n_parts:
            yield from chains_part(k - 1)
        if 0 <= k - 2 < n_parts:
            epilogue(parts[k - 2])
            yield


def _rwkv_kernel(*refs):
    for _ in _rwkv_steps(*refs):
        pass


def _rwkv(rkvg, wa, w2a2, vecs, ts):
    b, s, _ = rkvg.shape
    ones = jnp.asarray(
        (jnp.arange(PAIR)[:, None] // HEAD_DIM) == (jnp.arange(PAIR)[None, :] // HEAD_DIM), BF16)
    tri = jnp.asarray(jnp.arange(CHUNK)[:, None] >= jnp.arange(CHUNK)[None, :], BF16)
    return pl.pallas_call(
        _rwkv_kernel,
        grid=(b, s // ts),
        in_specs=[
            pl.BlockSpec((1, ts, 4 * D_RWKV), lambda bi, i: (bi, i, 0)),
            pl.BlockSpec((1, ts, 2 * LORA), lambda bi, i: (bi, i, 0)),
            pl.BlockSpec((2 * LORA, 2 * D_RWKV), lambda bi, i: (0, 0)),
            pl.BlockSpec((8, D_RWKV), lambda bi, i: (0, 0)),
            pl.BlockSpec((PAIR, PAIR), lambda bi, i: (0, 0)),
            pl.BlockSpec((CHUNK, CHUNK), lambda bi, i: (0, 0)),
        ],
        out_specs=pl.BlockSpec((1, ts, D_RWKV), lambda bi, i: (bi, i, 0)),
        out_shape=jax.ShapeDtypeStruct((b, s, D_RWKV), BF16),
        scratch_shapes=[
            pltpu.VMEM((N_PAIRS, HEAD_DIM, PAIR), F32),
            pltpu.VMEM((ts, D_RWKV), F32),
            pltpu.VMEM((ts, D_RWKV), F32),
            pltpu.VMEM((ts, D_RWKV), F32),
            pltpu.VMEM((ts, D_RWKV), F32),
            pltpu.VMEM((ts, D_RWKV), F32),
        ],
        compiler_params=_params(("arbitrary", "arbitrary")),
        name="rwkv7",
    )(rkvg, wa, w2a2, vecs, ones, tri)


def _attn_steps(q_ref, k_ref, v_ref, g_ref, qg_ref, kg_ref, ones_ref, bias_ref, o_ref,
                kp_s, vp_s, qn_s):
    i = pl.program_id(1)
    tq = q_ref.shape[1]
    s = k_ref.shape[1]
    n_chunks = tq // CHUNK
    ones = ones_ref[...]

    def head_norm(x, gain):
        sq = (x * x).astype(BF16)
        parts = [_dot(sq[:, p * PAIR:(p + 1) * PAIR], ones) for p in range(N_PAIRS)]
        ms = jnp.concatenate(parts, axis=1) * (1.0 / HEAD_DIM)
        return (x * lax.rsqrt(ms + RMS_EPS) * gain).astype(BF16)

    @pl.when(i == 0)
    def _():
        kp_s[0:WPAD, :] = jnp.zeros((WPAD, D_ATT), BF16)
        vp_s[0:WPAD, :] = jnp.zeros((WPAD, D_ATT), BF16)
        kp_s[WPAD:WPAD + s, :] = head_norm(k_ref[0].astype(F32), kg_ref[...])
        vp_s[WPAD:WPAD + s, :] = v_ref[0]

    lane = lax.broadcasted_iota(jnp.int32, (CHUNK, PAIR), 1)
    lane_lo = lane < HEAD_DIM
    jpos = lax.broadcasted_iota(jnp.int32, (CHUNK, WBAND), 1)
    qn_s[...] = head_norm(q_ref[0].astype(F32), qg_ref[...] * (HEAD_DIM ** -0.5 * LOG2E))
    pair_sl = [slice(p * PAIR, (p + 1) * PAIR) for p in range(N_PAIRS)]

    def scores(ci):
        win = pl.ds(pl.multiple_of(i * tq + ci * CHUNK, CHUNK), WBAND)
        return [_dot_nt(_stack2(qn_s[ci * CHUNK:(ci + 1) * CHUNK, sl], lane_lo), kp_s[win, sl])
                for sl in pair_sl]

    def finish(ci, sc):
        rows = slice(ci * CHUNK, (ci + 1) * CHUNK)
        q0 = i * tq + ci * CHUNK
        win = pl.ds(pl.multiple_of(q0, CHUNK), WBAND)
        valid = jpos >= WPAD - q0
        probs, invs = [], []
        for p in range(N_PAIRS):
            pes = []
            for e in range(2):
                se = sc[p][e * CHUNK:(e + 1) * CHUNK] + bias_ref[0, 2 * p + e]
                se = jnp.where(valid, se, MASK_VALUE)
                pe = jnp.exp2(se - jnp.max(se, axis=-1, keepdims=True))
                invs.append(1.0 / jnp.sum(pe, axis=-1, keepdims=True))
                pes.append(pe.astype(BF16))
            probs.append(jnp.concatenate(pes, axis=0))
        yield
        pv = [_dot(probs[p], vp_s[win, pair_sl[p]]) for p in range(N_PAIRS)]
        o = jnp.concatenate(
            [jnp.where(lane_lo, pv[p][0:CHUNK] * invs[2 * p], pv[p][CHUNK:2 * CHUNK] * invs[2 * p + 1])
             for p in range(N_PAIRS)], axis=1)
        g = g_ref[0, rows, :].astype(F32)
        o_ref[0, rows, :] = (o * (g * _sigmoid(g))).astype(BF16)
        yield

    sc_next = scores(0)
    yield
    for ci in range(n_chunks):
        sc = sc_next
        if ci + 1 < n_chunks:
            sc_next = scores(ci + 1)
            yield
        yield from finish(ci, sc)


def _attn_kernel(*refs):
    for _ in _attn_steps(*refs):
        pass


def _attn(att, q_g, k_g, bias_all, layer, tq):
    b, s, _ = att.shape
    ones = jnp.asarray(
        (jnp.arange(PAIR)[:, None] // HEAD_DIM) == (jnp.arange(PAIR)[None, :] // HEAD_DIM), BF16)
    qg = jnp.tile(q_g, N_HEADS).reshape(1, D_ATT)
    kg = jnp.tile(k_g, N_HEADS).reshape(1, D_ATT)
    return pl.pallas_call(
        _attn_kernel,
        grid=(b, s // tq),
        in_specs=[
            pl.BlockSpec((1, tq, D_ATT), lambda bi, i: (bi, i, 0)),
            pl.BlockSpec((1, s, D_ATT), lambda bi, i: (bi, 0, 1)),
            pl.BlockSpec((1, s, D_ATT), lambda bi, i: (bi, 0, 2)),
            pl.BlockSpec((1, tq, D_ATT), lambda bi, i: (bi, i, 3)),
            pl.BlockSpec((1, D_ATT), lambda bi, i: (0, 0)),
            pl.BlockSpec((1, D_ATT), lambda bi, i: (0, 0)),
            pl.BlockSpec((PAIR, PAIR), lambda bi, i: (0, 0)),
            pl.BlockSpec((1, N_HEADS, CHUNK, WBAND), lambda bi, i: (layer, 0, 0, 0)),
        ],
        out_specs=pl.BlockSpec((1, tq, D_ATT), lambda bi, i: (bi, i, 0)),
        out_shape=jax.ShapeDtypeStruct((b, s, D_ATT), BF16),
        scratch_shapes=[
            pltpu.VMEM((WPAD + s, D_ATT), BF16),
            pltpu.VMEM((WPAD + s, D_ATT), BF16),
            pltpu.VMEM((tq, D_ATT), BF16),
        ],
        compiler_params=_params(("arbitrary", "arbitrary")),
        name="band_attn",
    )(att, att, att, att, qg, kg, ones, bias_all)


def _mixer_kernel(rkvg_ref, wa_ref, w2a2_ref, vec_ref, ones_ref, tri_ref,
                  q_ref, k_ref, v_ref, g_ref, qg_ref, kg_ref, bias_ref,
                  yr_ref, ya_ref,
                  st_ref, lw_s, al_s, be_s, km_s, y_s, kp_s, vp_s, qn_s):
    streams = [
        _rwkv_steps(rkvg_ref, wa_ref, w2a2_ref, vec_ref, ones_ref, tri_ref, yr_ref,
                    st_ref, lw_s, al_s, be_s, km_s, y_s),
        _attn_steps(q_ref, k_ref, v_ref, g_ref, qg_ref, kg_ref, ones_ref, bias_ref, ya_ref,
                    kp_s, vp_s, qn_s),
    ]
    while streams:
        for stream, steps in list(zip(streams, MIXER_STEPS_PER_TURN)):
            for _ in range(steps):
                if stream in streams and next(stream, StopIteration) is StopIteration:
                    streams.remove(stream)


def _mixers(rkvg, wa, att, w2a2, vecs, q_g, k_g, bias_all, layer, ts):
    b, s, _ = rkvg.shape
    ones = jnp.asarray(
        (jnp.arange(PAIR)[:, None] // HEAD_DIM) == (jnp.arange(PAIR)[None, :] // HEAD_DIM), BF16)
    tri = jnp.asarray(jnp.arange(CHUNK)[:, None] >= jnp.arange(CHUNK)[None, :], BF16)
    qg = jnp.tile(q_g, N_HEADS).reshape(1, D_ATT)
    kg = jnp.tile(k_g, N_HEADS).reshape(1, D_ATT)
    const = lambda bi, i: (0, 0)
    return pl.pallas_call(
        _mixer_kernel,
        grid=(b, s // ts),
        in_specs=[
            pl.BlockSpec((1, ts, 4 * D_RWKV), lambda bi, i: (bi, i, 0)),
            pl.BlockSpec((1, ts, 2 * LORA), lambda bi, i: (bi, i, 0)),
            pl.BlockSpec((2 * LORA, 2 * D_RWKV), const),
            pl.BlockSpec((8, D_RWKV), const),
            pl.BlockSpec((PAIR, PAIR), const),
            pl.BlockSpec((CHUNK, CHUNK), const),
            pl.BlockSpec((1, ts, D_ATT), lambda bi, i: (bi, i, 0)),
            pl.BlockSpec((1, s, D_ATT), lambda bi, i: (bi, 0, 1)),
            pl.BlockSpec((1, s, D_ATT), lambda bi, i: (bi, 0, 2)),
            pl.BlockSpec((1, ts, D_ATT), lambda bi, i: (bi, i, 3)),
            pl.BlockSpec((1, D_ATT), const),
            pl.BlockSpec((1, D_ATT), const),
            pl.BlockSpec((1, N_HEADS, CHUNK, WBAND), lambda bi, i: (layer, 0, 0, 0)),
        ],
        out_specs=[
            pl.BlockSpec((1, ts, D_RWKV), lambda bi, i: (bi, i, 0)),
            pl.BlockSpec((1, ts, D_ATT), lambda bi, i: (bi, i, 0)),
        ],
        out_shape=[
            jax.ShapeDtypeStruct((b, s, D_RWKV), BF16),
            jax.ShapeDtypeStruct((b, s, D_ATT), BF16),
        ],
        scratch_shapes=[
            pltpu.VMEM((N_PAIRS, HEAD_DIM, PAIR), F32),
            pltpu.VMEM((ts, D_RWKV), F32),
            pltpu.VMEM((ts, D_RWKV), F32),
            pltpu.VMEM((ts, D_RWKV), F32),
            pltpu.VMEM((ts, D_RWKV), F32),
            pltpu.VMEM((ts, D_RWKV), F32),
            pltpu.VMEM((WPAD + s, D_ATT), BF16),
            pltpu.VMEM((WPAD + s, D_ATT), BF16),
            pltpu.VMEM((ts, D_ATT), BF16),
        ],
        compiler_params=_params(("arbitrary", "arbitrary")),
        name="mixers",
    )(rkvg, wa, w2a2, vecs, ones, tri, att, att, att, att, qg, kg, bias_all)


def _outproj_kernel(x_ref, yr_ref, ya_ref, mod_ref, w_ref, o_ref, wb_s):
    @pl.when(jnp.logical_and(pl.program_id(0) == 0, pl.program_id(1) == 0))
    def _():
        wb_s[...] = w_ref[...].astype(BF16)

    gate = mod_ref[0, :, 2 * D_MODEL:3 * D_MODEL]
    o_ref[0] = _gated_residual(x_ref[0], yr_ref[0], ya_ref[0], gate, wb_s)


def _outproj(x, yr, ya, mod_l, w_out_all, layer, tm):
    b, s, d = x.shape
    return pl.pallas_call(
        _outproj_kernel,
        grid=(b, s // tm),
        in_specs=[
            pl.BlockSpec((1, tm, d), lambda bi, i: (bi, i, 0)),
            pl.BlockSpec((1, tm, D_RWKV), lambda bi, i: (bi, i, 0)),
            pl.BlockSpec((1, tm, D_ATT), lambda bi, i: (bi, i, 0)),
            pl.BlockSpec((1, 1, 3 * d), lambda bi, i: (bi, 0, 0)),
            pl.BlockSpec((None, D_RWKV + D_ATT, d), lambda bi, i: (layer, 0, 0)),
        ],
        out_specs=pl.BlockSpec((1, tm, d), lambda bi, i: (bi, i, 0)),
        out_shape=jax.ShapeDtypeStruct((b, s, d), F32),
        scratch_shapes=[pltpu.VMEM((D_RWKV + D_ATT, d), BF16)],
        compiler_params=_params(("arbitrary", "arbitrary")),
        name="outproj",
    )(x, yr, ya, mod_l.reshape(b, 1, 3 * d), w_out_all)


def _tile(s, want):
    t = min(want, s)
    assert s % t == 0 and t % CHUNK == 0
    return t


def kernel(x, c, norm_g, w_ada, b_ada, w_in, mu_shift, w0, w2, a0, a2, k_k, k_a, r_k, lnx_g, lnx_b,
           q_norm_g, k_norm_g, rel_bias, w_out):
    n_layers = w_in.shape[0]
    s = x.shape[1]
    tm = _tile(s, INPROJ_ROWS)
    ts = _tile(s, MIXER_ROWS)
    tm_out = _tile(s, OUTPROJ_ROWS)
    mod = _ada_mod(c, w_ada, b_ada)
    bias = _bias_tables(rel_bias)
    zeros = jnp.zeros((LORA, D_RWKV), F32)
    prev = None
    for l in range(n_layers):
        if prev is None:
            rkvg, wa, att = _inproj(x, mod[l], norm_g[l], w_in, l, mu_shift[l], tm)
        else:
            x, rkvg, wa, att = _inproj(x, mod[l], norm_g[l], w_in, l, mu_shift[l], tm, prev=prev)
        w2a2 = jnp.concatenate([jnp.concatenate([w2[l], zeros], axis=1),
                                jnp.concatenate([zeros, a2[l]], axis=1)], axis=0).astype(BF16)
        vecs = jnp.stack([w0[l], a0[l], k_k[l], k_a[l], r_k[l].reshape(-1), lnx_g[l], lnx_b[l],
                          jnp.zeros((D_RWKV,), F32)])
        yr, ya = _mixers(rkvg, wa, att, w2a2, vecs, q_norm_g[l], k_norm_g[l], bias, l, ts)
        prev = (yr, ya, mod[l], w_out)
    return _outproj(x, yr, ya, mod[n_layers - 1], w_out, n_layers - 1, tm_out)
```

```python
import functools

import jax
import jax.numpy as jnp
from jax import lax
from jax.experimental import pallas as pl
from jax.experimental.pallas import tpu as pltpu

D_MODEL = 1024
CHUNK = 64
HEAD_DIM = 64
D_RWKV = 512
D_ATT = 512
N_HEADS = 8
LORA = 64
N_LEFT = 8
PAD = N_LEFT * CHUNK
WPAD = PAD + CHUNK
WBAND = WPAD + CHUNK
REL_CLIP = 128
N_REL = CHUNK + REL_CLIP
RMS_EPS = 1e-6
GN_EPS = 64e-5
D_SHIFT = 4 * D_RWKV + 2 * LORA
D_IN = D_SHIFT + 4 * D_ATT
PAIR = 2 * HEAD_DIM
N_PAIRS = N_HEADS // 2
RWKV_CHUNKS_PER_STEP = 4
MIXER_STEPS_PER_TURN = (1, 1)
INPROJ_ROWS = 512
INPROJ_PARTS = 2
MIXER_ROWS = 512
OUTPROJ_ROWS = 1024
LOG2E = 1.4426950408889634
EXP_NEG_HALF = 0.6065306597126334
MASK_VALUE = -1e30

VMEM_LIMIT_V7X = 56 * 1024 * 1024

F32 = jnp.float32
BF16 = jnp.bfloat16


def _dot(a, b):
    return jnp.dot(a, b, preferred_element_type=F32)


def _dot_nt(a, b):
    return lax.dot_general(a, b, (((1,), (1,)), ((), ())), preferred_element_type=F32)


def _sigmoid(x):
    return 0.5 * jnp.tanh(0.5 * x) + 0.5


def _split2(x):
    hi = x.astype(BF16)
    lo = (x - hi.astype(F32)).astype(BF16)
    return hi, lo


def _split3(x):
    hi = x.astype(BF16)
    r1 = x - hi.astype(F32)
    mid = r1.astype(BF16)
    lo = (r1 - mid.astype(F32)).astype(BF16)
    return hi, mid, lo


def _params(sem):
    return pltpu.CompilerParams(dimension_semantics=sem, vmem_limit_bytes=VMEM_LIMIT_V7X)


def _ada_kernel(c_ref, w_ref, b_ref, o_ref):
    c = c_ref[...]
    ca = c / (1.0 + jnp.exp(-c))
    o_ref[0] = _dot(ca.astype(BF16), w_ref[0].astype(BF16)) + b_ref[0]


def _ada_mod(c, w_ada, b_ada):
    n_layers, d, d3 = w_ada.shape
    b = c.shape[0]
    tn = 1024
    return pl.pallas_call(
        _ada_kernel,
        grid=(n_layers, d3 // tn),
        in_specs=[
            pl.BlockSpec((b, d), lambda l, j: (0, 0)),
            pl.BlockSpec((1, d, tn), lambda l, j: (l, 0, j)),
            pl.BlockSpec((1, 1, tn), lambda l, j: (l, 0, j)),
        ],
        out_specs=pl.BlockSpec((1, b, tn), lambda l, j: (l, 0, j)),
        out_shape=jax.ShapeDtypeStruct((n_layers, b, d3), F32),
        compiler_params=_params(("arbitrary", "arbitrary")),
        name="ada_mod",
    )(c, w_ada, b_ada.reshape(n_layers, 1, d3))


def _bias_kernel(tbl_ref, o_ref, tab_s):
    mi = lax.broadcasted_iota(jnp.int32, (N_REL, WBAND), 0)
    ji = lax.broadcasted_iota(jnp.int32, (N_REL, WBAND), 1)
    select = (jnp.clip(WPAD - ji, -(CHUNK - 1), REL_CLIP) + (CHUNK - 1) == mi).astype(BF16)
    hi, mid, lo = _split3(tbl_ref[0] * LOG2E)
    row0 = _dot(hi, select) + _dot(mid, select) + _dot(lo, select)
    masked = ji[0:N_HEADS] < WPAD - PAD
    for q in range(CHUNK):
        rolled = row0 if q == 0 else pltpu.roll(row0, q, 1)
        tab_s[q] = jnp.where(masked, MASK_VALUE, rolled)
    for h in range(N_HEADS):
        o_ref[0, h] = tab_s[:, h, :]


def _bias_tables(rel_bias):
    n_layers = rel_bias.shape[0]
    return pl.pallas_call(
        _bias_kernel,
        grid=(n_layers,),
        in_specs=[pl.BlockSpec((1, N_HEADS, N_REL), lambda l: (l, 0, 0))],
        out_specs=pl.BlockSpec((1, N_HEADS, CHUNK, WBAND), lambda l: (l, 0, 0, 0)),
        out_shape=jax.ShapeDtypeStruct((n_layers, N_HEADS, CHUNK, WBAND), F32),
        scratch_shapes=[pltpu.VMEM((CHUNK, N_HEADS, WBAND), F32)],
        compiler_params=_params(("arbitrary",)),
        name="bias_tables",
    )(rel_bias)


def _gated_residual(x, yr, ya, gate, w_out_ref):
    acc = _dot(yr, w_out_ref[0:D_RWKV, :]) + _dot(ya, w_out_ref[D_RWKV:D_RWKV + D_ATT, :])
    return x + gate * acc


def _inproj_kernel(*refs, fuse_prev):
    if fuse_prev:
        (x_ref, yr_ref, ya_ref, modp_ref, wo_ref, mod_ref, g_ref, w_ref, mu_ref,
         xo_ref, rkvg_ref, wa_ref, att_ref, carry_ref, wb_s, wob_s) = refs
    else:
        x_ref, mod_ref, g_ref, w_ref, mu_ref, rkvg_ref, wa_ref, att_ref, carry_ref, wb_s = refs
    i = pl.program_id(1)
    tm = x_ref.shape[1]

    @pl.when(jnp.logical_and(pl.program_id(0) == 0, i == 0))
    def _():
        wb_s[...] = w_ref[...].astype(BF16)
        if fuse_prev:
            wob_s[...] = wo_ref[...].astype(BF16)

    @pl.when(i == 0)
    def _():
        carry_ref[...] = jnp.zeros_like(carry_ref)

    shift = mod_ref[0, :, 0:D_MODEL]
    scale = mod_ref[0, :, D_MODEL:2 * D_MODEL]

    def front(rows):
        x = x_ref[0, rows, :]
        if fuse_prev:
            gate = modp_ref[0, :, 2 * D_MODEL:3 * D_MODEL]
            x = _gated_residual(x, yr_ref[0, rows, :], ya_ref[0, rows, :], gate, wob_s)
            xo_ref[0, rows, :] = x
        y = x * lax.rsqrt(jnp.mean(x * x, axis=-1, keepdims=True) + RMS_EPS)
        return ((y * g_ref[...]) * (1.0 + scale) + shift).astype(BF16)

    def back(rows, h, carry):
        n = rows.stop - rows.start
        p = _dot(h, wb_s[:, 0:D_SHIFT])
        row = lax.broadcasted_iota(jnp.int32, (n, D_SHIFT), 0)
        prev = jnp.where(row == 0, carry, pltpu.roll(p, 1, 0))
        ps = p + mu_ref[...] * (prev - p)
        rkvg_ref[0, rows, :] = ps[:, 0:4 * D_RWKV].astype(BF16)
        wa_ref[0, rows, :] = ps[:, 4 * D_RWKV:D_SHIFT].astype(BF16)
        att_ref[0, rows, :] = _dot(h, wb_s[:, D_SHIFT:D_IN]).astype(BF16)
        return p[n - 1:n, :]

    part = tm // INPROJ_PARTS
    parts = [slice(k * part, (k + 1) * part) for k in range(INPROJ_PARTS)]
    hs = [front(rows) for rows in parts]
    carry = carry_ref[...]
    for rows, h in zip(parts, hs):
        carry = back(rows, h, carry)
    carry_ref[...] = carry


def _inproj(x, mod_l, norm_g_l, w_in_all, layer, mu_l, tm, prev=None):
    b, s, d = x.shape
    row_spec = lambda width: pl.BlockSpec((1, tm, width), lambda bi, i: (bi, i, 0))
    mod_spec = pl.BlockSpec((1, 1, 3 * d), lambda bi, i: (bi, 0, 0))
    in_specs = [
        mod_spec,
        pl.BlockSpec((1, d), lambda bi, i: (0, 0)),
        pl.BlockSpec((None, d, D_IN), lambda bi, i: (layer, 0, 0)),
        pl.BlockSpec((1, D_SHIFT), lambda bi, i: (0, 0)),
    ]
    args = [mod_l.reshape(b, 1, 3 * d), norm_g_l.reshape(1, d), w_in_all, mu_l.reshape(1, D_SHIFT)]
    out_specs = [row_spec(4 * D_RWKV), row_spec(2 * LORA), row_spec(4 * D_ATT)]
    out_shape = [
        jax.ShapeDtypeStruct((b, s, 4 * D_RWKV), BF16),
        jax.ShapeDtypeStruct((b, s, 2 * LORA), BF16),
        jax.ShapeDtypeStruct((b, s, 4 * D_ATT), BF16),
    ]
    if prev is None:
        in_specs = [row_spec(d)] + in_specs
        args = [x] + args
    else:
        yr, ya, mod_prev, w_out_all = prev
        in_specs = [
            row_spec(d), row_spec(D_RWKV), row_spec(D_ATT), mod_spec,
            pl.BlockSpec((None, D_RWKV + D_ATT, d), lambda bi, i: (layer - 1, 0, 0)),
        ] + in_specs
        args = [x, yr, ya, mod_prev.reshape(b, 1, 3 * d), w_out_all] + args
        out_specs = [row_spec(d)] + out_specs
        out_shape = [jax.ShapeDtypeStruct((b, s, d), F32)] + out_shape
    return pl.pallas_call(
        functools.partial(_inproj_kernel, fuse_prev=prev is not None),
        grid=(b, s // tm),
        in_specs=in_specs,
        out_specs=out_specs,
        out_shape=out_shape,
        scratch_shapes=[pltpu.VMEM((1, D_SHIFT), F32), pltpu.VMEM((d, D_IN), BF16)]
        + ([] if prev is None else [pltpu.VMEM((D_RWKV + D_ATT, d), BF16)]),
        compiler_params=_params(("arbitrary", "arbitrary")),
        name="inproj",
    )(*args)


def _stack2(x, lane_lo):
    zero = jnp.zeros_like(x)
    return jnp.concatenate([jnp.where(lane_lo, x, zero), jnp.where(lane_lo, zero, x)], axis=0)


def _rwkv_steps(rkvg_ref, wa_ref, w2a2_ref, vec_ref, ones_ref, tri_ref, o_ref,
                st_ref, lw_s, al_s, be_s, km_s, y_s):
    i = pl.program_id(1)
    ts = rkvg_ref.shape[1]

    @pl.when(i == 0)
    def _():
        st_ref[...] = jnp.zeros_like(st_ref)

    w0 = vec_ref[0:1, :]
    a0 = vec_ref[1:2, :]
    k_k = vec_ref[2:3, :]
    k_a = vec_ref[3:4, :]
    r_k = vec_ref[4:5, :]
    lnx_g = vec_ref[5:6, :]
    lnx_b = vec_ref[6:7, :]
    ones = ones_ref[...]

    def group_sum(x, exact=False):
        hi = x.astype(BF16)
        lo = (x - hi.astype(F32)).astype(BF16) if exact else None
        parts = []
        for p in range(N_PAIRS):
            sl = slice(p * PAIR, (p + 1) * PAIR)
            acc = _dot(hi[:, sl], ones)
            parts.append(acc + _dot(lo[:, sl], ones) if exact else acc)
        return jnp.concatenate(parts, axis=1)

    def prologue(rows):
        k = rkvg_ref[0, rows, D_RWKV:2 * D_RWKV].astype(F32)
        wa = wa_ref[0, rows, :].astype(F32)
        lane_wa = lax.broadcasted_iota(jnp.int32, wa.shape, 1)
        tw = jnp.where(lane_wa < LORA, jnp.tanh(wa), wa).astype(BF16)
        za = _dot(tw, w2a2_ref[...])
        z = w0 + za[:, 0:D_RWKV]
        aa = a0 + za[:, D_RWKV:2 * D_RWKV]
        lw_s[rows, :] = -EXP_NEG_HALF * _sigmoid(z)
        a = _sigmoid(aa)
        kk = k * k_k
        kk = kk * lax.rsqrt(jnp.maximum(group_sum(kk * kk), 1e-24))
        al_s[rows, :] = -kk
        be_s[rows, :] = kk * a
        km_s[rows, :] = k * (1.0 + (a - 1.0) * k_a)

    tri = tri_ref[...]

    cat0 = functools.partial(jnp.concatenate, axis=0)
    cat1 = functools.partial(jnp.concatenate, axis=1)
    group = RWKV_CHUNKS_PER_STEP
    assert CHUNK == HEAD_DIM
    rowq = lax.broadcasted_iota(jnp.int32, (CHUNK, PAIR), 0)
    laneq = lax.broadcasted_iota(jnp.int32, (CHUNK, PAIR), 1)
    pair_lo = laneq < HEAD_DIM
    colq = laneq & (HEAD_DIM - 1)
    mq_strict = colq < rowq
    mq_incl = colq <= rowq
    pair_eye = colq == rowq
    eyeq = pair_eye.astype(F32)
    pair_bd = functools.partial(_stack2, lane_lo=pair_lo)

    def chains_part(gi):
        prep = []
        for ci in range(group):
            r0 = (gi * group + ci) * CHUNK
            rows = slice(r0, r0 + CHUNK)
            r = rkvg_ref[0, rows, 0:D_RWKV].astype(F32)
            v = rkvg_ref[0, rows, 2 * D_RWKV:3 * D_RWKV]
            lw = lw_s[rows, :]
            hi, lo = _split2(lw)
            cum = _dot(tri, hi) + _dot(tri, lo)
            w_inv = jnp.exp(-cum)
            w_all = jnp.exp(jnp.sum(lw, axis=0, keepdims=True))
            bt = be_s[rows, :] * w_inv
            kt = km_s[rows, :] * w_inv
            prep.append(dict(
                rows=rows, v=v,
                rt=r * jnp.exp(cum),
                at=(al_s[rows, :] * jnp.exp(cum - lw)).astype(BF16),
                bt=bt.astype(BF16),
                kt=kt.astype(BF16),
                bw=bt * w_all,
                kw=kt * w_all,
                w_all=w_all))
        chains = [(ci, q) for ci in range(group) for q in range(N_PAIRS)]
        n_ch = len(chains)
        sls = [slice(q * PAIR, (q + 1) * PAIR) for _, q in chains]
        at_c = [prep[ci]["at"][:, sl] for (ci, _), sl in zip(chains, sls)]
        rt_c = [prep[ci]["rt"][:, sl] for (ci, _), sl in zip(chains, sls)]
        v_c = [prep[ci]["v"][:, sl] for (ci, _), sl in zip(chains, sls)]
        amat = [_dot_nt(cat0([at_c[j], rt_c[j].astype(BF16)]),
                        cat0([pair_bd(prep[ci]["bt"][:, sls[j]]), pair_bd(prep[ci]["kt"][:, sls[j]])]))
                for j, (ci, _) in enumerate(chains)]
        yield
        n1 = [jnp.where(mq_strict, a[0:CHUNK, 0:PAIR], 0.0) for a in amat]
        a_ak = [jnp.where(mq_strict, a[0:CHUNK, PAIR:2 * PAIR], 0.0).astype(BF16) for a in amat]
        a_rb = [jnp.where(mq_incl, a[CHUNK:2 * CHUNK, 0:PAIR], 0.0).astype(BF16) for a in amat]
        a_rk = [jnp.where(mq_incl, a[CHUNK:2 * CHUNK, PAIR:2 * PAIR], 0.0).astype(BF16) for a in amat]
        n1b = [n.astype(BF16) for n in n1]
        npow = [_dot(n, pair_bd(n)) for n in n1b]
        av = [_dot(cat0([a_ak[j], a_rk[j]]), pair_bd(v_c[j])) for j in range(n_ch)]
        yield
        tmat = [eyeq + n for n in n1]
        for _ in range(4):
            npb = [n.astype(BF16) for n in npow]
            both = [_dot(cat0([t.astype(BF16), n]), pair_bd(n)) for t, n in zip(tmat, npb)]
            tmat = [t + b[0:CHUNK] for t, b in zip(tmat, both)]
            npow = [b[CHUNK:2 * CHUNK] for b in both]
            yield
        tmat = [t + _dot(t.astype(BF16), pair_bd(n.astype(BF16))) for t, n in zip(tmat, npow)]
        yield
        t2 = [_dot(tmat[j].astype(BF16), cat1([pair_bd(at_c[j]), pair_bd(av[j][0:CHUNK].astype(BF16))]))
              for j in range(n_ch)]
        a_pr = [t[:, 0:PAIR].astype(BF16) for t in t2]
        uv = [t[:, PAIR:2 * PAIR].astype(BF16) for t in t2]
        yield
        r2 = [_dot(a_rb[j], cat1([pair_bd(a_pr[j]), pair_bd(uv[j])])) for j in range(n_ch)]
        r_pr = [(rt_c[j] + r2[j][:, 0:PAIR]).astype(BF16) for j in range(n_ch)]
        y0 = [av[j][CHUNK:2 * CHUNK] + r2[j][:, PAIR:2 * PAIR] for j in range(n_ch)]
        yield
        gmat, hmat = [], []
        for j, (ci, _) in enumerate(chains):
            lt = cat0([prep[ci]["bw"][:, sls[j]], prep[ci]["kw"][:, sls[j]]]).T.astype(BF16)
            rgt = cat0([cat1([a_pr[j], uv[j]]), cat1([jnp.zeros_like(v_c[j]), v_c[j]])])
            gh = _dot(lt, rgt)
            g_sbs = jnp.where(pair_lo, gh[0:HEAD_DIM, 0:PAIR], gh[HEAD_DIM:PAIR, 0:PAIR])
            gmat.append((g_sbs + jnp.where(pair_eye, prep[ci]["w_all"][:, sls[j]], 0.0)).astype(BF16))
            hmat.append(jnp.where(pair_lo, gh[0:HEAD_DIM, PAIR:2 * PAIR], gh[HEAD_DIM:PAIR, PAIR:2 * PAIR]))
        yield
        for ci in range(group):
            ys = []
            for p in range(N_PAIRS):
                j = ci * N_PAIRS + p
                seq = _dot(cat0([r_pr[j], gmat[j]]), pair_bd(st_ref[p].astype(BF16)))
                ys.append(seq[0:CHUNK] + y0[j])
                st_ref[p] = seq[CHUNK:CHUNK + HEAD_DIM] + hmat[j]
            y_s[prep[ci]["rows"], :] = cat1(ys)
            yield

    def epilogue(rows):
        y = y_s[rows, :]
        r = rkvg_ref[0, rows, 0:D_RWKV].astype(F32)
        v = rkvg_ref[0, rows, 2 * D_RWKV:3 * D_RWKV].astype(F32)
        g = rkvg_ref[0, rows, 3 * D_RWKV:4 * D_RWKV].astype(F32)
        inv_n = 1.0 / HEAD_DIM
        yc = y - group_sum(y, exact=True) * inv_n
        yn = yc * lax.rsqrt(group_sum(yc * yc) * inv_n + GN_EPS)
        out = yn * lnx_g + lnx_b + group_sum(r * km_s[rows, :] * r_k) * v
        o_ref[0, rows, :] = (out * (g * _sigmoid(g))).astype(BF16)

    part_rows = group * CHUNK
    n_parts = ts // part_rows
    parts = [slice(k * part_rows, (k + 1) * part_rows) for k in range(n_parts)]
    for k in range(n_parts + 2):
        if k < n_parts:
            prologue(parts[k])
            yield
        if 0 <= k - 1 < n_parts:
            yield from chains_part(k - 1)
        if 0 <= k - 2 < n_parts:
            epilogue(parts[k - 2])
            yield


def _attn_steps(q_ref, k_ref, v_ref, g_ref, qg_ref, kg_ref, ones_ref, bias_ref, o_ref,
                kp_s, vp_s, qn_s):
    i = pl.program_id(1)
    tq = q_ref.shape[1]
    s = k_ref.shape[1]
    n_chunks = tq // CHUNK
    ones = ones_ref[...]

    def head_norm(x, gain):
        sq = (x * x).astype(BF16)
        parts = [_dot(sq[:, p * PAIR:(p + 1) * PAIR], ones) for p in range(N_PAIRS)]
        ms = jnp.concatenate(parts, axis=1) * (1.0 / HEAD_DIM)
        return (x * lax.rsqrt(ms + RMS_EPS) * gain).astype(BF16)

    @pl.when(i == 0)
    def _():
        kp_s[0:WPAD, :] = jnp.zeros((WPAD, D_ATT), BF16)
        vp_s[0:WPAD, :] = jnp.zeros((WPAD, D_ATT), BF16)
        kp_s[WPAD:WPAD + s, :] = head_norm(k_ref[0].astype(F32), kg_ref[...])
        vp_s[WPAD:WPAD + s, :] = v_ref[0]

    lane = lax.broadcasted_iota(jnp.int32, (CHUNK, PAIR), 1)
    lane_lo = lane < HEAD_DIM
    jpos = lax.broadcasted_iota(jnp.int32, (CHUNK, WBAND), 1)
    qn_s[...] = head_norm(q_ref[0].astype(F32), qg_ref[...] * (HEAD_DIM ** -0.5 * LOG2E))
    pair_sl = [slice(p * PAIR, (p + 1) * PAIR) for p in range(N_PAIRS)]

    def scores(ci):
        win = pl.ds(pl.multiple_of(i * tq + ci * CHUNK, CHUNK), WBAND)
        return [_dot_nt(_stack2(qn_s[ci * CHUNK:(ci + 1) * CHUNK, sl], lane_lo), kp_s[win, sl])
                for sl in pair_sl]

    def finish(ci, sc):
        rows = slice(ci * CHUNK, (ci + 1) * CHUNK)
        q0 = i * tq + ci * CHUNK
        win = pl.ds(pl.multiple_of(q0, CHUNK), WBAND)
        valid = jpos >= WPAD - q0
        probs, invs = [], []
        for p in range(N_PAIRS):
            pes = []
            for e in range(2):
                se = sc[p][e * CHUNK:(e + 1) * CHUNK] + bias_ref[0, 2 * p + e]
                se = jnp.where(valid, se, MASK_VALUE)
                pe = jnp.exp2(se - jnp.max(se, axis=-1, keepdims=True))
                invs.append(1.0 / jnp.sum(pe, axis=-1, keepdims=True))
                pes.append(pe.astype(BF16))
            probs.append(jnp.concatenate(pes, axis=0))
        yield
        pv = [_dot(probs[p], vp_s[win, pair_sl[p]]) for p in range(N_PAIRS)]
        o = jnp.concatenate(
            [jnp.where(lane_lo, pv[p][0:CHUNK] * invs[2 * p], pv[p][CHUNK:2 * CHUNK] * invs[2 * p + 1])
             for p in range(N_PAIRS)], axis=1)
        g = g_ref[0, rows, :].astype(F32)
        o_ref[0, rows, :] = (o * (g * _sigmoid(g))).astype(BF16)
        yield

    sc_next = scores(0)
    yield
    for ci in range(n_chunks):
        sc = sc_next
        if ci + 1 < n_chunks:
            sc_next = scores(ci + 1)
            yield
        yield from finish(ci, sc)


def _mixer_kernel(rkvg_ref, wa_ref, w2a2_ref, vec_ref, ones_ref, tri_ref,
                  q_ref, k_ref, v_ref, g_ref, qg_ref, kg_ref, bias_ref,
                  yr_ref, ya_ref,
                  st_ref, lw_s, al_s, be_s, km_s, y_s, kp_s, vp_s, qn_s):
    streams = [
        _rwkv_steps(rkvg_ref, wa_ref, w2a2_ref, vec_ref, ones_ref, tri_ref, yr_ref,
                    st_ref, lw_s, al_s, be_s, km_s, y_s),
        _attn_steps(q_ref, k_ref, v_ref, g_ref, qg_ref, kg_ref, ones_ref, bias_ref, ya_ref,
                    kp_s, vp_s, qn_s),
    ]
    while streams:
        for stream, steps in list(zip(streams, MIXER_STEPS_PER_TURN)):
            for _ in range(steps):
                if stream in streams and next(stream, StopIteration) is StopIteration:
                    streams.remove(stream)


def _mixers(rkvg, wa, att, w2a2, vecs, q_g, k_g, bias_all, layer, ts):
    b, s, _ = rkvg.shape
    ones = jnp.asarray(
        (jnp.arange(PAIR)[:, None] // HEAD_DIM) == (jnp.arange(PAIR)[None, :] // HEAD_DIM), BF16)
    tri = jnp.asarray(jnp.arange(CHUNK)[:, None] >= jnp.arange(CHUNK)[None, :], BF16)
    qg = jnp.tile(q_g, N_HEADS).reshape(1, D_ATT)
    kg = jnp.tile(k_g, N_HEADS).reshape(1, D_ATT)
    const = lambda bi, i: (0, 0)
    return pl.pallas_call(
        _mixer_kernel,
        grid=(b, s // ts),
        in_specs=[
            pl.BlockSpec((1, ts, 4 * D_RWKV), lambda bi, i: (bi, i, 0)),
            pl.BlockSpec((1, ts, 2 * LORA), lambda bi, i: (bi, i, 0)),
            pl.BlockSpec((2 * LORA, 2 * D_RWKV), const),
            pl.BlockSpec((8, D_RWKV), const),
            pl.BlockSpec((PAIR, PAIR), const),
            pl.BlockSpec((CHUNK, CHUNK), const),
            pl.BlockSpec((1, ts, D_ATT), lambda bi, i: (bi, i, 0)),
            pl.BlockSpec((1, s, D_ATT), lambda bi, i: (bi, 0, 1)),
            pl.BlockSpec((1, s, D_ATT), lambda bi, i: (bi, 0, 2)),
            pl.BlockSpec((1, ts, D_ATT), lambda bi, i: (bi, i, 3)),
            pl.BlockSpec((1, D_ATT), const),
            pl.BlockSpec((1, D_ATT), const),
            pl.BlockSpec((1, N_HEADS, CHUNK, WBAND), lambda bi, i: (layer, 0, 0, 0)),
        ],
        out_specs=[
            pl.BlockSpec((1, ts, D_RWKV), lambda bi, i: (bi, i, 0)),
            pl.BlockSpec((1, ts, D_ATT), lambda bi, i: (bi, i, 0)),
        ],
        out_shape=[
            jax.ShapeDtypeStruct((b, s, D_RWKV), BF16),
            jax.ShapeDtypeStruct((b, s, D_ATT), BF16),
        ],
        scratch_shapes=[
            pltpu.VMEM((N_PAIRS, HEAD_DIM, PAIR), F32),
            pltpu.VMEM((ts, D_RWKV), F32),
            pltpu.VMEM((ts, D_RWKV), F32),
            pltpu.VMEM((ts, D_RWKV), F32),
            pltpu.VMEM((ts, D_RWKV), F32),
            pltpu.VMEM((ts, D_RWKV), F32),
            pltpu.VMEM((WPAD + s, D_ATT), BF16),
            pltpu.VMEM((WPAD + s, D_ATT), BF16),
            pltpu.VMEM((ts, D_ATT), BF16),
        ],
        compiler_params=_params(("arbitrary", "arbitrary")),
        name="mixers",
    )(rkvg, wa, w2a2, vecs, ones, tri, att, att, att, att, qg, kg, bias_all)


def _outproj_kernel(x_ref, yr_ref, ya_ref, mod_ref, w_ref, o_ref, wb_s):
    @pl.when(jnp.logical_and(pl.program_id(0) == 0, pl.program_id(1) == 0))
    def _():
        wb_s[...] = w_ref[...].astype(BF16)

    gate = mod_ref[0, :, 2 * D_MODEL:3 * D_MODEL]
    o_ref[0] = _gated_residual(x_ref[0], yr_ref[0], ya_ref[0], gate, wb_s)


def _outproj(x, yr, ya, mod_l, w_out_all, layer, tm):
    b, s, d = x.shape
    return pl.pallas_call(
        _outproj_kernel,
        grid=(b, s // tm),
        in_specs=[
            pl.BlockSpec((1, tm, d), lambda bi, i: (bi, i, 0)),
            pl.BlockSpec((1, tm, D_RWKV), lambda bi, i: (bi, i, 0)),
            pl.BlockSpec((1, tm, D_ATT), lambda bi, i: (bi, i, 0)),
            pl.BlockSpec((1, 1, 3 * d), lambda bi, i: (bi, 0, 0)),
            pl.BlockSpec((None, D_RWKV + D_ATT, d), lambda bi, i: (layer, 0, 0)),
        ],
        out_specs=pl.BlockSpec((1, tm, d), lambda bi, i: (bi, i, 0)),
        out_shape=jax.ShapeDtypeStruct((b, s, d), F32),
        scratch_shapes=[pltpu.VMEM((D_RWKV + D_ATT, d), BF16)],
        compiler_params=_params(("arbitrary", "arbitrary")),
        name="outproj",
    )(x, yr, ya, mod_l.reshape(b, 1, 3 * d), w_out_all)


def _tile(s, want):
    t = min(want, s)
    assert s % t == 0 and t % CHUNK == 0
    return t


def kernel(x, c, norm_g, w_ada, b_ada, w_in, mu_shift, w0, w2, a0, a2, k_k, k_a, r_k, lnx_g, lnx_b,
           q_norm_g, k_norm_g, rel_bias, w_out):
    n_layers = w_in.shape[0]
    s = x.shape[1]
    tm = _tile(s, INPROJ_ROWS)
    ts = _tile(s, MIXER_ROWS)
    tm_out = _tile(s, OUTPROJ_ROWS)
    mod = _ada_mod(c, w_ada, b_ada)
    bias = _bias_tables(rel_bias)
    zeros = jnp.zeros((LORA, D_RWKV), F32)
    prev = None
    for l in range(n_layers):
        if prev is None:
            rkvg, wa, att = _inproj(x, mod[l], norm_g[l], w_in, l, mu_shift[l], tm)
        else:
            x, rkvg, wa, att = _inproj(x, mod[l], norm_g[l], w_in, l, mu_shift[l], tm, prev=prev)
        w2a2 = jnp.concatenate([jnp.concatenate([w2[l], zeros], axis=1),
                                jnp.concatenate([zeros, a2[l]], axis=1)], axis=0).astype(BF16)
        vecs = jnp.stack([w0[l], a0[l], k_k[l], k_a[l], r_k[l].reshape(-1), lnx_g[l], lnx_b[l],
                          jnp.zeros((D_RWKV,), F32)])
        yr, ya = _mixers(rkvg, wa, att, w2a2, vecs, q_norm_g[l], k_norm_g[l], bias, l, ts)
        prev = (yr, ya, mod[l], w_out)
    return _outproj(x, yr, ya, mod[n_layers - 1], w_out, n_layers - 1, tm_out)
```

```python
import functools

import jax
import jax.numpy as jnp
from jax import lax
from jax.experimental import pallas as pl
from jax.experimental.pallas import tpu as pltpu

D_MODEL = 1024
CHUNK = 64
HEAD_DIM = 64
D_RWKV = 512
D_ATT = 512
N_HEADS = 8
LORA = 64
N_LEFT = 8
PAD = N_LEFT * CHUNK
WPAD = PAD + CHUNK
WBAND = WPAD + CHUNK
REL_CLIP = 128
N_REL = CHUNK + REL_CLIP
RMS_EPS = 1e-6
GN_EPS = 64e-5
D_SHIFT = 4 * D_RWKV + 2 * LORA
D_IN = D_SHIFT + 4 * D_ATT
PAIR = 2 * HEAD_DIM
N_PAIRS = N_HEADS // 2
RWKV_CHUNKS_PER_STEP = 4
MIXER_STEPS_PER_TURN = (1, 1)
INPROJ_ROWS = 512
INPROJ_PARTS = 2
MIXER_ROWS = 512
OUTPROJ_ROWS = 1024
LOG2E = 1.4426950408889634
EXP_NEG_HALF = 0.6065306597126334
MASK_VALUE = -1e30

VMEM_LIMIT_V7X = 56 * 1024 * 1024

F32 = jnp.float32
BF16 = jnp.bfloat16


def _dot(a, b):
    return jnp.dot(a, b, preferred_element_type=F32)


def _dot_nt(a, b):
    return lax.dot_general(a, b, (((1,), (1,)), ((), ())), preferred_element_type=F32)


def _sigmoid(x):
    return 0.5 * jnp.tanh(0.5 * x) + 0.5


def _split2(x):
    hi = x.astype(BF16)
    lo = (x - hi.astype(F32)).astype(BF16)
    return hi, lo


def _split3(x):
    hi = x.astype(BF16)
    r1 = x - hi.astype(F32)
    mid = r1.astype(BF16)
    lo = (r1 - mid.astype(F32)).astype(BF16)
    return hi, mid, lo


def _params(sem):
    return pltpu.CompilerParams(dimension_semantics=sem, vmem_limit_bytes=VMEM_LIMIT_V7X)


def _ada_kernel(c_ref, w_ref, b_ref, o_ref):
    c = c_ref[...]
    ca = c / (1.0 + jnp.exp(-c))
    o_ref[0] = _dot(ca.astype(BF16), w_ref[0].astype(BF16)) + b_ref[0]


def _ada_mod(c, w_ada, b_ada):
    n_layers, d, d3 = w_ada.shape
    b = c.shape[0]
    tn = 1024
    return pl.pallas_call(
        _ada_kernel,
        grid=(n_layers, d3 // tn),
        in_specs=[
            pl.BlockSpec((b, d), lambda l, j: (0, 0)),
            pl.BlockSpec((1, d, tn), lambda l, j: (l, 0, j)),
            pl.BlockSpec((1, 1, tn), lambda l, j: (l, 0, j)),
        ],
        out_specs=pl.BlockSpec((1, b, tn), lambda l, j: (l, 0, j)),
        out_shape=jax.ShapeDtypeStruct((n_layers, b, d3), F32),
        compiler_params=_params(("arbitrary", "arbitrary")),
        name="ada_mod",
    )(c, w_ada, b_ada.reshape(n_layers, 1, d3))


def _bias_kernel(tbl_ref, o_ref, tab_s):
    mi = lax.broadcasted_iota(jnp.int32, (N_REL, WBAND), 0)
    ji = lax.broadcasted_iota(jnp.int32, (N_REL, WBAND), 1)
    select = (jnp.clip(WPAD - ji, -(CHUNK - 1), REL_CLIP) + (CHUNK - 1) == mi).astype(BF16)
    hi, mid, lo = _split3(tbl_ref[0] * LOG2E)
    row0 = _dot(hi, select) + _dot(mid, select) + _dot(lo, select)
    masked = ji[0:N_HEADS] < WPAD - PAD
    for q in range(CHUNK):
        rolled = row0 if q == 0 else pltpu.roll(row0, q, 1)
        tab_s[q] = jnp.where(masked, MASK_VALUE, rolled)
    for h in range(N_HEADS):
        o_ref[0, h] = tab_s[:, h, :]


def _bias_tables(rel_bias):
    n_layers = rel_bias.shape[0]
    return pl.pallas_call(
        _bias_kernel,
        grid=(n_layers,),
        in_specs=[pl.BlockSpec((1, N_HEADS, N_REL), lambda l: (l, 0, 0))],
        out_specs=pl.BlockSpec((1, N_HEADS, CHUNK, WBAND), lambda l: (l, 0, 0, 0)),
        out_shape=jax.ShapeDtypeStruct((n_layers, N_HEADS, CHUNK, WBAND), F32),
        scratch_shapes=[pltpu.VMEM((CHUNK, N_HEADS, WBAND), F32)],
        compiler_params=_params(("arbitrary",)),
        name="bias_tables",
    )(rel_bias)


def _gated_residual(x, yr, ya, gate, w_out_ref):
    acc = _dot(yr, w_out_ref[0:D_RWKV, :]) + _dot(ya, w_out_ref[D_RWKV:D_RWKV + D_ATT, :])
    return x + gate * acc


def _inproj_kernel(*refs, fuse_prev):
    if fuse_prev:
        (x_ref, yr_ref, ya_ref, modp_ref, wo_ref, mod_ref, g_ref, w_ref, mu_ref,
         xo_ref, rkvg_ref, wa_ref, att_ref, carry_ref, wb_s, wob_s) = refs
    else:
        x_ref, mod_ref, g_ref, w_ref, mu_ref, rkvg_ref, wa_ref, att_ref, carry_ref, wb_s = refs
    i = pl.program_id(1)
    tm = x_ref.shape[1]

    @pl.when(jnp.logical_and(pl.program_id(0) == 0, i == 0))
    def _():
        wb_s[...] = w_ref[...].astype(BF16)
        if fuse_prev:
            wob_s[...] = wo_ref[...].astype(BF16)

    @pl.when(i == 0)
    def _():
        carry_ref[...] = jnp.zeros_like(carry_ref)

    shift = mod_ref[0, :, 0:D_MODEL]
    scale = mod_ref[0, :, D_MODEL:2 * D_MODEL]

    def front(rows):
        x = x_ref[0, rows, :]
        if fuse_prev:
            gate = modp_ref[0, :, 2 * D_MODEL:3 * D_MODEL]
            x = _gated_residual(x, yr_ref[0, rows, :], ya_ref[0, rows, :], gate, wob_s)
            xo_ref[0, rows, :] = x
        y = x * lax.rsqrt(jnp.mean(x * x, axis=-1, keepdims=True) + RMS_EPS)
        return ((y * g_ref[...]) * (1.0 + scale) + shift).astype(BF16)

    def back(rows, h, carry):
        n = rows.stop - rows.start
        p = _dot(h, wb_s[:, 0:D_SHIFT])
        row = lax.broadcasted_iota(jnp.int32, (n, D_SHIFT), 0)
        prev = jnp.where(row == 0, carry, pltpu.roll(p, 1, 0))
        ps = p + mu_ref[...] * (prev - p)
        rkvg_ref[0, rows, :] = ps[:, 0:4 * D_RWKV].astype(BF16)
        wa_ref[0, rows, :] = ps[:, 4 * D_RWKV:D_SHIFT].astype(BF16)
        att_ref[0, rows, :] = _dot(h, wb_s[:, D_SHIFT:D_IN]).astype(BF16)
        return p[n - 1:n, :]

    part = tm // INPROJ_PARTS
    parts = [slice(k * part, (k + 1) * part) for k in range(INPROJ_PARTS)]
    hs = [front(rows) for rows in parts]
    carry = carry_ref[...]
    for rows, h in zip(parts, hs):
        carry = back(rows, h, carry)
    carry_ref[...] = carry


def _inproj(x, mod_l, norm_g_l, w_in_all, layer, mu_l, tm, prev=None):
    b, s, d = x.shape
    row_spec = lambda width: pl.BlockSpec((1, tm, width), lambda bi, i: (bi, i, 0))
    mod_spec = pl.BlockSpec((1, 1, 3 * d), lambda bi, i: (bi, 0, 0))
    in_specs = [
        mod_spec,
        pl.BlockSpec((1, d), lambda bi, i: (0, 0)),
        pl.BlockSpec((None, d, D_IN), lambda bi, i: (layer, 0, 0)),
        pl.BlockSpec((1, D_SHIFT), lambda bi, i: (0, 0)),
    ]
    args = [mod_l.reshape(b, 1, 3 * d), norm_g_l.reshape(1, d), w_in_all, mu_l.reshape(1, D_SHIFT)]
    out_specs = [row_spec(4 * D_RWKV), row_spec(2 * LORA), row_spec(4 * D_ATT)]
    out_shape = [
        jax.ShapeDtypeStruct((b, s, 4 * D_RWKV), BF16),
        jax.ShapeDtypeStruct((b, s, 2 * LORA), BF16),
        jax.ShapeDtypeStruct((b, s, 4 * D_ATT), BF16),
    ]
    if prev is None:
        in_specs = [row_spec(d)] + in_specs
        args = [x] + args
    else:
        yr, ya, mod_prev, w_out_all = prev
        in_specs = [
            row_spec(d), row_spec(D_RWKV), row_spec(D_ATT), mod_spec,
            pl.BlockSpec((None, D_RWKV + D_ATT, d), lambda bi, i: (layer - 1, 0, 0)),
        ] + in_specs
        args = [x, yr, ya, mod_prev.reshape(b, 1, 3 * d), w_out_all] + args
        out_specs = [row_spec(d)] + out_specs
        out_shape = [jax.ShapeDtypeStruct((b, s, d), F32)] + out_shape
    return pl.pallas_call(
        functools.partial(_inproj_kernel, fuse_prev=prev is not None),
        grid=(b, s // tm),
        in_specs=in_specs,
        out_specs=out_specs,
        out_shape=out_shape,
        scratch_shapes=[pltpu.VMEM((1, D_SHIFT), F32), pltpu.VMEM((d, D_IN), BF16)]
        + ([] if prev is None else [pltpu.VMEM((D_RWKV + D_ATT, d), BF16)]),
        compiler_params=_params(("arbitrary", "arbitrary")),
        name="inproj",
    )(*args)


def _stack2(x, lane_lo):
    zero = jnp.zeros_like(x)
    return jnp.concatenate([jnp.where(lane_lo, x, zero), jnp.where(lane_lo, zero, x)], axis=0)


def _rwkv_steps(rkvg_ref, wa_ref, w2a2_ref, vec_ref, ones_ref, tri_ref, o_ref,
                st_ref, lw_s, al_s, be_s, km_s, y_s):
    i = pl.program_id(1)
    ts = rkvg_ref.shape[1]

    @pl.when(i == 0)
    def _():
        st_ref[...] = jnp.zeros_like(st_ref)

    w0 = vec_ref[0:1, :]
    a0 = vec_ref[1:2, :]
    k_k = vec_ref[2:3, :]
    k_a = vec_ref[3:4, :]
    r_k = vec_ref[4:5, :]
    lnx_g = vec_ref[5:6, :]
    lnx_b = vec_ref[6:7, :]
    ones = ones_ref[...]

    def group_sum(x, exact=False):
        hi = x.astype(BF16)
        lo = (x - hi.astype(F32)).astype(BF16) if exact else None
        parts = []
        for p in range(N_PAIRS):
            sl = slice(p * PAIR, (p + 1) * PAIR)
            acc = _dot(hi[:, sl], ones)
            parts.append(acc + _dot(lo[:, sl], ones) if exact else acc)
        return jnp.concatenate(parts, axis=1)

    def prologue(rows):
        k = rkvg_ref[0, rows, D_RWKV:2 * D_RWKV].astype(F32)
        wa = wa_ref[0, rows, :].astype(F32)
        lane_wa = lax.broadcasted_iota(jnp.int32, wa.shape, 1)
        tw = jnp.where(lane_wa < LORA, jnp.tanh(wa), wa).astype(BF16)
        za = _dot(tw, w2a2_ref[...])
        z = w0 + za[:, 0:D_RWKV]
        aa = a0 + za[:, D_RWKV:2 * D_RWKV]
        lw_s[rows, :] = -EXP_NEG_HALF * _sigmoid(z)
        a = _sigmoid(aa)
        kk = k * k_k
        kk = kk * lax.rsqrt(jnp.maximum(group_sum(kk * kk), 1e-24))
        al_s[rows, :] = -kk
        be_s[rows, :] = kk * a
        km_s[rows, :] = k * (1.0 + (a - 1.0) * k_a)

    tri = tri_ref[...]

    cat0 = functools.partial(jnp.concatenate, axis=0)
    cat1 = functools.partial(jnp.concatenate, axis=1)
    group = RWKV_CHUNKS_PER_STEP
    assert CHUNK == HEAD_DIM
    rowq = lax.broadcasted_iota(jnp.int32, (CHUNK, PAIR), 0)
    laneq = lax.broadcasted_iota(jnp.int32, (CHUNK, PAIR), 1)
    pair_lo = laneq < HEAD_DIM
    colq = laneq & (HEAD_DIM - 1)
    mq_strict = colq < rowq
    mq_incl = colq <= rowq
    pair_eye = colq == rowq
    eyeq = pair_eye.astype(F32)
    pair_bd = functools.partial(_stack2, lane_lo=pair_lo)

    def chains_part(gi):
        prep = []
        for ci in range(group):
            r0 = (gi * group + ci) * CHUNK
            rows = slice(r0, r0 + CHUNK)
            r = rkvg_ref[0, rows, 0:D_RWKV].astype(F32)
            v = rkvg_ref[0, rows, 2 * D_RWKV:3 * D_RWKV]
            lw = lw_s[rows, :]
            hi, lo = _split2(lw)
            cum = _dot(tri, hi) + _dot(tri, lo)
            w_inv = jnp.exp(-cum)
            w_all = jnp.exp(jnp.sum(lw, axis=0, keepdims=True))
            bt = be_s[rows, :] * w_inv
            kt = km_s[rows, :] * w_inv
            prep.append(dict(
                rows=rows, v=v,
                rt=r * jnp.exp(cum),
                at=(al_s[rows, :] * jnp.exp(cum - lw)).astype(BF16),
                bt=bt.astype(BF16),
                kt=kt.astype(BF16),
                bw=bt * w_all,
                kw=kt * w_all,
                w_all=w_all))
        chains = [(ci, q) for ci in range(group) for q in range(N_PAIRS)]
        n_ch = len(chains)
        sls = [slice(q * PAIR, (q + 1) * PAIR) for _, q in chains]
        at_c = [prep[ci]["at"][:, sl] for (ci, _), sl in zip(chains, sls)]
        rt_c = [prep[ci]["rt"][:, sl] for (ci, _), sl in zip(chains, sls)]
        v_c = [prep[ci]["v"][:, sl] for (ci, _), sl in zip(chains, sls)]
        amat = [_dot_nt(cat0([at_c[j], rt_c[j].astype(BF16)]),
                        cat0([pair_bd(prep[ci]["bt"][:, sls[j]]), pair_bd(prep[ci]["kt"][:, sls[j]])]))
                for j, (ci, _) in enumerate(chains)]
        yield
        n1 = [jnp.where(mq_strict, a[0:CHUNK, 0:PAIR], 0.0) for a in amat]
        a_ak = [jnp.where(mq_strict, a[0:CHUNK, PAIR:2 * PAIR], 0.0).astype(BF16) for a in amat]
        a_rb = [jnp.where(mq_incl, a[CHUNK:2 * CHUNK, 0:PAIR], 0.0).astype(BF16) for a in amat]
        a_rk = [jnp.where(mq_incl, a[CHUNK:2 * CHUNK, PAIR:2 * PAIR], 0.0).astype(BF16) for a in amat]
        n1b = [n.astype(BF16) for n in n1]
        npow = [_dot(n, pair_bd(n)) for n in n1b]
        av = [_dot(cat0([a_ak[j], a_rk[j]]), pair_bd(v_c[j])) for j in range(n_ch)]
        yield
        tmat = [eyeq + n for n in n1]
        for _ in range(4):
            npb = [n.astype(BF16) for n in npow]
            both = [_dot(cat0([t.astype(BF16), n]), pair_bd(n)) for t, n in zip(tmat, npb)]
            tmat = [t + b[0:CHUNK] for t, b in zip(tmat, both)]
            npow = [b[CHUNK:2 * CHUNK] for b in both]
            yield
        tmat = [t + _dot(t.astype(BF16), pair_bd(n.astype(BF16))) for t, n in zip(tmat, npow)]
        yield
        t2 = [_dot(tmat[j].astype(BF16), cat1([pair_bd(at_c[j]), pair_bd(av[j][0:CHUNK].astype(BF16))]))
              for j in range(n_ch)]
        a_pr = [t[:, 0:PAIR].astype(BF16) for t in t2]
        uv = [t[:, PAIR:2 * PAIR].astype(BF16) for t in t2]
        yield
        r2 = [_dot(a_rb[j], cat1([pair_bd(a_pr[j]), pair_bd(uv[j])])) for j in range(n_ch)]
        r_pr = [(rt_c[j] + r2[j][:, 0:PAIR]).astype(BF16) for j in range(n_ch)]
        y0 = [av[j][CHUNK:2 * CHUNK] + r2[j][:, PAIR:2 * PAIR] for j in range(n_ch)]
        yield
        gmat, hmat = [], []
        for j, (ci, _) in enumerate(chains):
            lt = cat0([prep[ci]["bw"][:, sls[j]], prep[ci]["kw"][:, sls[j]]]).T.astype(BF16)
            rgt = cat0([cat1([a_pr[j], uv[j]]), cat1([jnp.zeros_like(v_c[j]), v_c[j]])])
            gh = _dot(lt, rgt)
            g_sbs = jnp.where(pair_lo, gh[0:HEAD_DIM, 0:PAIR], gh[HEAD_DIM:PAIR, 0:PAIR])
            gmat.append((g_sbs + jnp.where(pair_eye, prep[ci]["w_all"][:, sls[j]], 0.0)).astype(BF16))
            hmat.append(jnp.where(pair_lo, gh[0:HEAD_DIM, PAIR:2 * PAIR], gh[HEAD_DIM:PAIR, PAIR:2 * PAIR]))
        yield
        for ci in range(group):
            ys = []
            for p in range(N_PAIRS):
                j = ci * N_PAIRS + p
                seq = _dot(cat0([r_pr[j], gmat[j]]), pair_bd(st_ref[p].astype(BF16)))
                ys.append(seq[0:CHUNK] + y0[j])
                st_ref[p] = seq[CHUNK:CHUNK + HEAD_DIM] + hmat[j]
            y_s[prep[ci]["rows"], :] = cat1(ys)
            yield

    def epilogue(rows):
        y = y_s[rows, :]
        r = rkvg_ref[0, rows, 0:D_RWKV].astype(F32)
        v = rkvg_ref[0, rows, 2 * D_RWKV:3 * D_RWKV].astype(F32)
        g = rkvg_ref[0, rows, 3 * D_RWKV:4 * D_RWKV].astype(F32)
        inv_n = 1.0 / HEAD_DIM
        yc = y - group_sum(y, exact=True) * inv_n
        yn = yc * lax.rsqrt(group_sum(yc * yc) * inv_n + GN_EPS)
        out = yn * lnx_g + lnx_b + group_sum(r * km_s[rows, :] * r_k) * v
        o_ref[0, rows, :] = (out * (g * _sigmoid(g))).astype(BF16)

    part_rows = group * CHUNK
    n_parts = ts // part_rows
    parts = [slice(k * part_rows, (k + 1) * part_rows) for k in range(n_parts)]
    for k in range(n_parts + 2):
        if k < n_parts:
            prologue(parts[k])
            yield
        if 0 <= k - 1 < n_parts:
            yield from chains_part(k - 1)
        if 0 <= k - 2 < n_parts:
            epilogue(parts[k - 2])
            yield


def _attn_steps(q_ref, k_ref, v_ref, g_ref, qg_ref, kg_ref, ones_ref, bias_ref, o_ref,
                kp_s, vp_s, qn_s):
    i = pl.program_id(1)
    tq = q_ref.shape[1]
    n_chunks = tq // CHUNK
    ones = ones_ref[...]

    def head_norm(x, gain):
        sq = (x * x).astype(BF16)
        parts = [_dot(sq[:, p * PAIR:(p + 1) * PAIR], ones) for p in range(N_PAIRS)]
        ms = jnp.concatenate(parts, axis=1) * (1.0 / HEAD_DIM)
        return (x * lax.rsqrt(ms + RMS_EPS) * gain).astype(BF16)

    @pl.when(i == 0)
    def _():
        kp_s[0:WPAD, :] = jnp.zeros((WPAD, D_ATT), BF16)
        vp_s[0:WPAD, :] = jnp.zeros((WPAD, D_ATT), BF16)

    own = pl.ds(pl.multiple_of(WPAD + i * tq, CHUNK), tq)
    kp_s[own, :] = head_norm(k_ref[0].astype(F32), kg_ref[...])
    vp_s[own, :] = v_ref[0]

    lane = lax.broadcasted_iota(jnp.int32, (CHUNK, PAIR), 1)
    lane_lo = lane < HEAD_DIM
    jpos = lax.broadcasted_iota(jnp.int32, (CHUNK, WBAND), 1)
    qn_s[...] = head_norm(q_ref[0].astype(F32), qg_ref[...] * (HEAD_DIM ** -0.5 * LOG2E))
    pair_sl = [slice(p * PAIR, (p + 1) * PAIR) for p in range(N_PAIRS)]

    def scores(ci):
        win = pl.ds(pl.multiple_of(i * tq + ci * CHUNK, CHUNK), WBAND)
        return [_dot_nt(_stack2(qn_s[ci * CHUNK:(ci + 1) * CHUNK, sl], lane_lo), kp_s[win, sl])
                for sl in pair_sl]

    def finish(ci, sc):
        rows = slice(ci * CHUNK, (ci + 1) * CHUNK)
        q0 = i * tq + ci * CHUNK
        win = pl.ds(pl.multiple_of(q0, CHUNK), WBAND)
        valid = jpos >= WPAD - q0
        probs, invs = [], []
        for p in range(N_PAIRS):
            pes = []
            for e in range(2):
                se = sc[p][e * CHUNK:(e + 1) * CHUNK] + bias_ref[0, 2 * p + e]
                se = jnp.where(valid, se, MASK_VALUE)
                pe = jnp.exp2(se - jnp.max(se, axis=-1, keepdims=True))
                invs.append(1.0 / jnp.sum(pe, axis=-1, keepdims=True))
                pes.append(pe.astype(BF16))
            probs.append(jnp.concatenate(pes, axis=0))
        yield
        pv = [_dot(probs[p], vp_s[win, pair_sl[p]]) for p in range(N_PAIRS)]
        o = jnp.concatenate(
            [jnp.where(lane_lo, pv[p][0:CHUNK] * invs[2 * p], pv[p][CHUNK:2 * CHUNK] * invs[2 * p + 1])
             for p in range(N_PAIRS)], axis=1)
        g = g_ref[0, rows, :].astype(F32)
        o_ref[0, rows, :] = (o * (g * _sigmoid(g))).astype(BF16)
        yield

    sc_next = scores(0)
    yield
    for ci in range(n_chunks):
        sc = sc_next
        if ci + 1 < n_chunks:
            sc_next = scores(ci + 1)
            yield
        yield from finish(ci, sc)


def _mixer_kernel(rkvg_ref, wa_ref, w2a2_ref, vec_ref, ones_ref, tri_ref,
                  q_ref, k_ref, v_ref, g_ref, qg_ref, kg_ref, bias_ref,
                  yr_ref, ya_ref,
                  st_ref, lw_s, al_s, be_s, km_s, y_s, kp_s, vp_s, qn_s):
    streams = [
        _rwkv_steps(rkvg_ref, wa_ref, w2a2_ref, vec_ref, ones_ref, tri_ref, yr_ref,
                    st_ref, lw_s, al_s, be_s, km_s, y_s),
        _attn_steps(q_ref, k_ref, v_ref, g_ref, qg_ref, kg_ref, ones_ref, bias_ref, ya_ref,
                    kp_s, vp_s, qn_s),
    ]
    while streams:
        for stream, steps in list(zip(streams, MIXER_STEPS_PER_TURN)):
            for _ in range(steps):
                if stream in streams and next(stream, StopIteration) is StopIteration:
                    streams.remove(stream)


def _mixers(rkvg, wa, att, w2a2, vecs, q_g, k_g, bias_all, layer, ts):
    b, s, _ = rkvg.shape
    ones = jnp.asarray(
        (jnp.arange(PAIR)[:, None] // HEAD_DIM) == (jnp.arange(PAIR)[None, :] // HEAD_DIM), BF16)
    tri = jnp.asarray(jnp.arange(CHUNK)[:, None] >= jnp.arange(CHUNK)[None, :], BF16)
    qg = jnp.tile(q_g, N_HEADS).reshape(1, D_ATT)
    kg = jnp.tile(k_g, N_HEADS).reshape(1, D_ATT)
    const = lambda bi, i: (0, 0)
    return pl.pallas_call(
        _mixer_kernel,
        grid=(b, s // ts),
        in_specs=[
            pl.BlockSpec((1, ts, 4 * D_RWKV), lambda bi, i: (bi, i, 0)),
            pl.BlockSpec((1, ts, 2 * LORA), lambda bi, i: (bi, i, 0)),
            pl.BlockSpec((2 * LORA, 2 * D_RWKV), const),
            pl.BlockSpec((8, D_RWKV), const),
            pl.BlockSpec((PAIR, PAIR), const),
            pl.BlockSpec((CHUNK, CHUNK), const),
            pl.BlockSpec((1, ts, D_ATT), lambda bi, i: (bi, i, 0)),
            pl.BlockSpec((1, ts, D_ATT), lambda bi, i: (bi, i, 1)),
            pl.BlockSpec((1, ts, D_ATT), lambda bi, i: (bi, i, 2)),
            pl.BlockSpec((1, ts, D_ATT), lambda bi, i: (bi, i, 3)),
            pl.BlockSpec((1, D_ATT), const),
            pl.BlockSpec((1, D_ATT), const),
            pl.BlockSpec((1, N_HEADS, CHUNK, WBAND), lambda bi, i: (layer, 0, 0, 0)),
        ],
        out_specs=[
            pl.BlockSpec((1, ts, D_RWKV), lambda bi, i: (bi, i, 0)),
            pl.BlockSpec((1, ts, D_ATT), lambda bi, i: (bi, i, 0)),
        ],
        out_shape=[
            jax.ShapeDtypeStruct((b, s, D_RWKV), BF16),
            jax.ShapeDtypeStruct((b, s, D_ATT), BF16),
        ],
        scratch_shapes=[
            pltpu.VMEM((N_PAIRS, HEAD_DIM, PAIR), F32),
            pltpu.VMEM((ts, D_RWKV), F32),
            pltpu.VMEM((ts, D_RWKV), F32),
            pltpu.VMEM((ts, D_RWKV), F32),
            pltpu.VMEM((ts, D_RWKV), F32),
            pltpu.VMEM((ts, D_RWKV), F32),
            pltpu.VMEM((WPAD + s, D_ATT), BF16),
            pltpu.VMEM((WPAD + s, D_ATT), BF16),
            pltpu.VMEM((ts, D_ATT), BF16),
        ],
        compiler_params=_params(("arbitrary", "arbitrary")),
        name="mixers",
    )(rkvg, wa, w2a2, vecs, ones, tri, att, att, att, att, qg, kg, bias_all)


def _outproj_kernel(x_ref, yr_ref, ya_ref, mod_ref, w_ref, o_ref, wb_s):
    @pl.when(jnp.logical_and(pl.program_id(0) == 0, pl.program_id(1) == 0))
    def _():
        wb_s[...] = w_ref[...].astype(BF16)

    gate = mod_ref[0, :, 2 * D_MODEL:3 * D_MODEL]
    o_ref[0] = _gated_residual(x_ref[0], yr_ref[0], ya_ref[0], gate, wb_s)


def _outproj(x, yr, ya, mod_l, w_out_all, layer, tm):
    b, s, d = x.shape
    return pl.pallas_call(
        _outproj_kernel,
        grid=(b, s // tm),
        in_specs=[
            pl.BlockSpec((1, tm, d), lambda bi, i: (bi, i, 0)),
            pl.BlockSpec((1, tm, D_RWKV), lambda bi, i: (bi, i, 0)),
            pl.BlockSpec((1, tm, D_ATT), lambda bi, i: (bi, i, 0)),
            pl.BlockSpec((1, 1, 3 * d), lambda bi, i: (bi, 0, 0)),
            pl.BlockSpec((None, D_RWKV + D_ATT, d), lambda bi, i: (layer, 0, 0)),
        ],
        out_specs=pl.BlockSpec((1, tm, d), lambda bi, i: (bi, i, 0)),
        out_shape=jax.ShapeDtypeStruct((b, s, d), F32),
        scratch_shapes=[pltpu.VMEM((D_RWKV + D_ATT, d), BF16)],
        compiler_params=_params(("arbitrary", "arbitrary")),
        name="outproj",
    )(x, yr, ya, mod_l.reshape(b, 1, 3 * d), w_out_all)


def _tile(s, want):
    t = min(want, s)
    assert s % t == 0 and t % CHUNK == 0
    return t


def kernel(x, c, norm_g, w_ada, b_ada, w_in, mu_shift, w0, w2, a0, a2, k_k, k_a, r_k, lnx_g, lnx_b,
           q_norm_g, k_norm_g, rel_bias, w_out):
    n_layers = w_in.shape[0]
    s = x.shape[1]
    tm = _tile(s, INPROJ_ROWS)
    ts = _tile(s, MIXER_ROWS)
    tm_out = _tile(s, OUTPROJ_ROWS)
    mod = _ada_mod(c, w_ada, b_ada)
    bias = _bias_tables(rel_bias)
    zeros = jnp.zeros((LORA, D_RWKV), F32)
    prev = None
    for l in range(n_layers):
        if prev is None:
            rkvg, wa, att = _inproj(x, mod[l], norm_g[l], w_in, l, mu_shift[l], tm)
        else:
            x, rkvg, wa, att = _inproj(x, mod[l], norm_g[l], w_in, l, mu_shift[l], tm, prev=prev)
        w2a2 = jnp.concatenate([jnp.concatenate([w2[l], zeros], axis=1),
                                jnp.concatenate([zeros, a2[l]], axis=1)], axis=0).astype(BF16)
        vecs = jnp.stack([w0[l], a0[l], k_k[l], k_a[l], r_k[l].reshape(-1), lnx_g[l], lnx_b[l],
                          jnp.zeros((D_RWKV,), F32)])
        yr, ya = _mixers(rkvg, wa, att, w2a2, vecs, q_norm_g[l], k_norm_g[l], bias, l, ts)
        prev = (yr, ya, mod[l], w_out)
    return _outproj(x, yr, ya, mod[n_layers - 1], w_out, n_layers - 1, tm_out)
```
